```python
import math
import jax, jax.numpy as jnp
from jax import lax
import numpy as np

D_MODEL = 1024
BATCH = 16
SEQ = 2048
DEPTH = 2

N_MIXERS = 2
N_A_LAYERS = (DEPTH + N_MIXERS - 1) // N_MIXERS
N_B_LAYERS = DEPTH // N_MIXERS

A_HEADS = D_MODEL // 128
A_HEAD_DIM = 128
A_V_DIM = 128
Q_RANK = 256
KV_RANK = 256
IDX_HEADS = 8
IDX_DIM = 64
TOPK_MAX = 256
Q_BLOCK = 128
A_IN = Q_RANK + KV_RANK + IDX_DIM + IDX_HEADS

NUM_BUCKETS = 32
MAX_DISTANCE = 128

G_HEADS = 4
G_KD = D_MODEL // 2
G_VD = D_MODEL
G_DK = G_KD // G_HEADS
G_DV = G_VD // G_HEADS
G_RANK = 16
GATE_TAU = 16.0
CHUNK = 64
B_IN = 2 * G_KD + 2 * G_VD + G_RANK

D_FF = 2816
CONV_WIDTH = 3

EPS = 1e-6

kernel_name = "hybrid_dsa_gla_convffn"


def rmsnorm(x, g):
    xf = x.astype(jnp.float32)
    y = xf * lax.rsqrt(jnp.mean(xf * xf, axis=-1, keepdims=True) + EPS)
    return (y * g.astype(jnp.float32)).astype(x.dtype)


def t5_bucket(dist):
    n = jnp.maximum(dist, 0)
    exact = NUM_BUCKETS // 2
    log_ratio = jnp.log(jnp.maximum(n, 1).astype(jnp.float32) / exact) / math.log(MAX_DISTANCE / exact)
    large = exact + (log_ratio * (NUM_BUCKETS - exact)).astype(jnp.int32)
    return jnp.where(n < exact, n, jnp.minimum(large, NUM_BUCKETS - 1))


def dsa_mixer(h, rel_bias, w_in, g_cq, g_ckv, w_uq, w_uk, w_uv, w_qi, w_o):
    B, T, _ = h.shape
    c_q, c_kv, k_idx, w_idx = jnp.split(
        h @ w_in, [Q_RANK, Q_RANK + KV_RANK, Q_RANK + KV_RANK + IDX_DIM], axis=-1)
    c_q = rmsnorm(c_q, g_cq)
    c_kv = rmsnorm(c_kv, g_ckv)
    q = (c_q @ w_uq).reshape(B, T, A_HEADS, A_HEAD_DIM)
    q_lat = jnp.einsum('bthd,hdc->bthc', q, w_uk) * (A_HEAD_DIM ** -0.5)
    q_idx = (c_q @ w_qi).reshape(B, T, IDX_HEADS, IDX_DIM)
    w_idx = w_idx * (IDX_HEADS ** -0.5 * IDX_DIM ** -0.5)

    n_sel = min(TOPK_MAX, T // 4)
    n_blk = T // Q_BLOCK
    key_pos = jnp.arange(T, dtype=jnp.int32)

    def to_blocks(a):
        return a.reshape(B, n_blk, Q_BLOCK, *a.shape[2:]).swapaxes(0, 1)

    def attend_block(args):
        qb_lat, qb_idx, wb, start = args
        q_pos = start + jnp.arange(Q_BLOCK, dtype=jnp.int32)
        rel = jax.nn.relu(jnp.einsum('bqhd,bsd->bqhs', qb_idx, k_idx))
        score = jnp.einsum('bqh,bqhs->bqs', wb, rel).astype(jnp.float32)
        score = jnp.where(key_pos[None, None, :] <= q_pos[None, :, None], score, -jnp.inf)
        _, sel = lax.top_k(score, n_sel)
        kv_sel = jax.vmap(lambda a, i: a[i])(c_kv, sel.reshape(B, -1))
        kv_sel = kv_sel.reshape(B, Q_BLOCK, n_sel, KV_RANK)
        dist = q_pos[None, :, None] - sel
        bias = rel_bias[t5_bucket(dist)].transpose(0, 1, 3, 2)
        logits = (jnp.einsum('bqhc,bqkc->bqhk', qb_lat, kv_sel).astype(jnp.float32)
                  + bias.astype(jnp.float32))
        logits = jnp.where((dist >= 0)[:, :, None, :], logits, -jnp.inf)
        p = jax.nn.softmax(logits, axis=-1).astype(kv_sel.dtype)
        return jnp.einsum('bqhk,bqkc->bqhc', p, kv_sel)

    starts = jnp.arange(n_blk, dtype=jnp.int32) * Q_BLOCK
    o_lat = lax.map(attend_block, (to_blocks(q_lat), to_blocks(q_idx), to_blocks(w_idx), starts))
    o_lat = o_lat.swapaxes(0, 1).reshape(B, T, A_HEADS, KV_RANK)
    o = jnp.einsum('bthc,hcv->bthv', o_lat, w_uv).reshape(B, T, A_HEADS * A_V_DIM)
    return o @ w_o


def gla_mixer(h, w_in, w_g2, b_g, g_norm, w_o):
    B, T, _ = h.shape
    q, k, v, r, g_lr = jnp.split(
        h @ w_in, [G_KD, 2 * G_KD, 2 * G_KD + G_VD, 2 * G_KD + 2 * G_VD], axis=-1)
    log_a = jax.nn.log_sigmoid((g_lr @ w_g2 + b_g).astype(jnp.float32)) / GATE_TAU
    n_chk = T // CHUNK

    def heads_chunks(a, d):
        return a.astype(jnp.float32).reshape(B, n_chk, CHUNK, G_HEADS, d).transpose(1, 0, 3, 2, 4)

    qc = heads_chunks(q * (G_DK ** -0.5), G_DK)
    kc = heads_chunks(k, G_DK)
    vc = heads_chunks(v, G_DV)
    gc = heads_chunks(log_a, G_DK)
    idx = jnp.arange(CHUNK)
    causal = (idx[:, None] >= idx[None, :])[None, None, :, :, None]

    def step(S, inp):
        q_c, k_c, v_c, g_c = inp
        b = jnp.cumsum(g_c, axis=2)
        o_inter = jnp.einsum('bhik,bhkv->bhiv', q_c * jnp.exp(b), S)
        diff = jnp.where(causal, b[:, :, :, None, :] - b[:, :, None, :, :], -jnp.inf)
        attn = jnp.einsum('bhik,bhjk,bhijk->bhij', q_c, k_c, jnp.exp(diff))
        o_intra = jnp.einsum('bhij,bhjv->bhiv', attn, v_c)
        b_last = b[:, :, -1:, :]
        S = (S * jnp.exp(b_last[:, :, 0, :])[..., None]
             + jnp.einsum('bhjk,bhjv->bhkv', k_c * jnp.exp(b_last - b), v_c))
        return S, o_inter + o_intra

    S0 = jnp.zeros((B, G_HEADS, G_DK, G_DV), jnp.float32)
    _, o = lax.scan(step, S0, (qc, kc, vc, gc))
    o = o.transpose(1, 0, 3, 2, 4).reshape(B, T, G_HEADS, G_DV)
    o = rmsnorm(o, g_norm.reshape(G_HEADS, G_DV)).reshape(B, T, G_VD).astype(h.dtype)
    return (o * jax.nn.silu(r)) @ w_o


def conv_ffn(h, w_in, conv_w, conv_b, w_out):
    u, v = jnp.split(h @ w_in, 2, axis=-1)
    u = lax.conv_general_dilated(
        u, conv_w[:, None, :], window_strides=(1,), padding=[(CONV_WIDTH - 1, 0)],
        dimension_numbers=('NWC', 'WIO', 'NWC'), feature_group_count=D_FF) + conv_b
    return (jax.nn.gelu(u, approximate=False) * v) @ w_out


def setup_inputs(seed: int = 0) -> dict:
    key = jax.random.key(seed)
    ks = iter(jax.random.split(key, 40))
    f32 = jnp.float32

    def w(shape, fan_in):
        return jax.random.normal(next(ks), shape, f32) * fan_in ** -0.5

    def gain(shape):
        return 1.0 + 0.05 * jax.random.normal(next(ks), shape, f32)

    def small(shape, s=0.02):
        return s * jax.random.normal(next(ks), shape, f32)

    nA, nB = N_A_LAYERS, N_B_LAYERS
    return {
        "x": jax.random.normal(next(ks), (BATCH, SEQ, D_MODEL), f32),
        "c": jax.random.normal(next(ks), (BATCH, D_MODEL), f32),
        "rel_bias": small((NUM_BUCKETS, A_HEADS), 0.2),
        "a_w_in": w((nA, D_MODEL, A_IN), D_MODEL),
        "a_g_cq": gain((nA, Q_RANK)),
        "a_g_ckv": gain((nA, KV_RANK)),
        "a_w_uq": w((nA, Q_RANK, A_HEADS * A_HEAD_DIM), Q_RANK),
        "a_w_uk": w((nA, A_HEADS, A_HEAD_DIM, KV_RANK), A_HEAD_DIM),
        "a_w_uv": w((nA, A_HEADS, KV_RANK, A_V_DIM), KV_RANK),
        "a_w_qi": w((nA, Q_RANK, IDX_HEADS * IDX_DIM), Q_RANK),
        "a_w_o": w((nA, A_HEADS * A_V_DIM, D_MODEL), A_HEADS * A_V_DIM),
        "b_w_in": w((nB, D_MODEL, B_IN), D_MODEL),
        "b_w_g2": w((nB, G_RANK, G_KD), G_RANK),
        "b_b_g": small((nB, G_KD)),
        "b_g_norm": gain((nB, G_VD)),
        "b_w_o": w((nB, G_VD, D_MODEL), G_VD),
        "ada_w": w((DEPTH, D_MODEL, 6 * D_MODEL), D_MODEL),
        "ada_b": small((DEPTH, 6 * D_MODEL)),
        "g_mix": gain((DEPTH, D_MODEL)),
        "g_ffn": gain((DEPTH, D_MODEL)),
        "f_w_in": w((DEPTH, D_MODEL, 2 * D_FF), D_MODEL),
        "f_conv_w": w((DEPTH, CONV_WIDTH, D_FF), CONV_WIDTH),
        "f_conv_b": small((DEPTH, D_FF)),
        "f_w_out": w((DEPTH, D_FF, D_MODEL), D_FF),
        "g_final": gain((D_MODEL,)),
    }


def reference(x, c, rel_bias, a_w_in, a_g_cq, a_g_ckv, a_w_uq, a_w_uk, a_w_uv, a_w_qi, a_w_o,
              b_w_in, b_w_g2, b_b_g, b_g_norm, b_w_o, ada_w, ada_b, g_mix, g_ffn,
              f_w_in, f_conv_w, f_conv_b, f_w_out, g_final):
    cond = jax.nn.silu(c)
    for i in range(DEPTH):
        mod = cond @ ada_w[i] + ada_b[i]
        sh_m, sc_m, gt_m, sh_f, sc_f, gt_f = jnp.split(mod[:, None, :], 6, axis=-1)
        h = rmsnorm(x, g_mix[i]) * (1 + sc_m) + sh_m
        j = i // N_MIXERS
        if i % N_MIXERS == 0:
            y = dsa_mixer(h, rel_bias, a_w_in[j], a_g_cq[j], a_g_ckv[j], a_w_uq[j],
                          a_w_uk[j], a_w_uv[j], a_w_qi[j], a_w_o[j])
        else:
            y = gla_mixer(h, b_w_in[j], b_w_g2[j], b_b_g[j], b_g_norm[j], b_w_o[j])
        x = x + gt_m * y
        h = rmsnorm(x, g_ffn[i]) * (1 + sc_f) + sh_f
        x = x + gt_f * conv_ffn(h, f_w_in[i], f_conv_w[i], f_conv_b[i], f_w_out[i])
    return rmsnorm(x, g_final)
```

```python
import functools
import math

import numpy as np
import jax
import jax.numpy as jnp
from jax import lax
from jax.experimental import pallas as pl
from jax.experimental.pallas import tpu as pltpu

F32 = jnp.float32
BF16 = jnp.bfloat16
I32 = jnp.int32

D_MODEL = 1024
A_HEADS = 8
A_HEAD_DIM = 128
A_V_DIM = 128
Q_RANK = 256
KV_RANK = 256
IDX_HEADS = 8
IDX_DIM = 64
TOPK_MAX = 256
NUM_BUCKETS = 32
MAX_DISTANCE = 128
G_HEADS = 4
G_KD = 512
G_VD = 1024
G_DK = G_KD // G_HEADS
G_DV = G_VD // G_HEADS
G_RANK = 16
GATE_TAU = 16.0
D_FF = 2816
EPS = 1e-6

LANES = 128
VMEM_LIMIT_BYTES = 56 * 1024 * 1024

ROW_TILE = 512
ATT_TILE = 256
GLA_TILE = 512
GLA_CHUNK = 128
FF_SPLIT = 2
NEG_INF = float("-inf")


def _cparams(sem):
    return pltpu.CompilerParams(dimension_semantics=sem, vmem_limit_bytes=VMEM_LIMIT_BYTES)


def _dot(a, b):
    return lax.dot_general(a, b, (((1,), (0,)), ((), ())), preferred_element_type=F32)


def _dot_nt(a, b):
    return lax.dot_general(a, b, (((1,), (1,)), ((), ())), preferred_element_type=F32)


def _split2(a):
    hi = a.astype(BF16)
    lo = (a - hi.astype(F32)).astype(BF16)
    return hi, lo


def _split3(a):
    p1 = a.astype(BF16)
    r1 = a - p1.astype(F32)
    p2 = r1.astype(BF16)
    p3 = (r1 - p2.astype(F32)).astype(BF16)
    return p1, p2, p3


def _dot_hp(a, b):
    ah, al = _split2(a)
    bh, bl = _split2(b)
    return _dot(ah, bh) + (_dot(ah, bl) + _dot(al, bh))


def _hi_lo_concat(a, order):
    hi = a.astype(BF16).astype(F32)
    parts = (hi, a - hi)
    return jnp.concatenate([parts[o] for o in order], axis=1).astype(BF16)


def _rms(x):
    return x * lax.rsqrt(jnp.mean(x * x, axis=-1, keepdims=True) + EPS)


def _norm_mod(x, g, sc, sh):
    return (_rms(x) * g) * (1.0 + sc) + sh


def _mod_kernel(c_ref, w_ref, b_ref, o_ref):
    c = c_ref[...]
    cond = c * jax.nn.sigmoid(c)
    o_ref[0] = _dot_hp(cond, w_ref[0]) + b_ref[0]


def _modulation(c, ada_w, ada_b):
    depth, d, six_d = ada_w.shape
    b = c.shape[0]
    n_col = six_d // d
    return pl.pallas_call(
        _mod_kernel,
        grid=(depth, n_col),
        in_specs=[
            pl.BlockSpec((b, d), lambda l, j: (0, 0)),
            pl.BlockSpec((1, d, d), lambda l, j: (l, 0, j)),
            pl.BlockSpec((1, 1, d), lambda l, j: (l, 0, j)),
        ],
        out_specs=pl.BlockSpec((1, b, d), lambda l, j: (l, 0, j)),
        out_shape=jax.ShapeDtypeStruct((depth, b, six_d), F32),
        compiler_params=_cparams(("arbitrary", "arbitrary")),
        name="adaln_mod",
    )(c, ada_w, ada_b.reshape(depth, 1, six_d))


HP_COLS = 384


def _dsa_pre_kernel(x_ref, mod_ref, g_ref, whp_ref, wkv_ref, gcq_ref, gckv_ref, wuq_ref,
                    wuk_ref, wqi_ref, qlat_ref, qi3_ref, ki3_ref, widx_ref, ckv_ref):
    mod = mod_ref[0]
    h = _norm_mod(x_ref[0], g_ref[...], mod[1:2], mod[0:1])
    hp = _dot_hp(h, whp_ref[...])
    c_q = _rms(hp[:, :Q_RANK]) * gcq_ref[...]
    k_idx = hp[:, Q_RANK:Q_RANK + IDX_DIM]
    w_idx = hp[:, Q_RANK + IDX_DIM:Q_RANK + IDX_DIM + IDX_HEADS]
    c_kv = _rms(_dot(h.astype(BF16), wkv_ref[...])) * gckv_ref[...]
    ckv_ref[0] = c_kv.astype(BF16)
    widx_ref[0] = w_idx * (IDX_HEADS ** -0.5 * IDX_DIM ** -0.5)
    ki3_ref[0] = _hi_lo_concat(k_idx, (0, 1, 0))

    q = _dot(c_q.astype(BF16), wuq_ref[...])
    for hh in range(A_HEADS):
        qh = q[:, hh * A_HEAD_DIM:(hh + 1) * A_HEAD_DIM].astype(BF16)
        qlat_ref[0, hh] = (_dot(qh, wuk_ref[hh]) * (A_HEAD_DIM ** -0.5)).astype(BF16)

    q_idx = _dot_hp(c_q, wqi_ref[...])
    for hh in range(IDX_HEADS):
        qi3_ref[0, hh] = _hi_lo_concat(q_idx[:, hh * IDX_DIM:(hh + 1) * IDX_DIM], (0, 0, 1))


def _dsa_pre(x, mod, g_mix, w_in, g_cq, g_ckv, w_uq, w_uk, w_qi):
    b, t, d = x.shape
    tm = min(ROW_TILE, t)
    n_hp = Q_RANK + IDX_DIM + IDX_HEADS
    w_hp = jnp.concatenate(
        [w_in[:, :Q_RANK], w_in[:, Q_RANK + KV_RANK:], jnp.zeros((d, HP_COLS - n_hp), F32)], axis=1)
    w_kv = w_in[:, Q_RANK:Q_RANK + KV_RANK].astype(BF16)
    const = lambda *shape: pl.BlockSpec(shape, lambda i, j: (0,) * len(shape))
    return pl.pallas_call(
        _dsa_pre_kernel,
        grid=(b, t // tm),
        in_specs=[
            pl.BlockSpec((1, tm, d), lambda i, j: (i, j, 0)),
            pl.BlockSpec((1, 6, d), lambda i, j: (i, 0, 0)),
            const(1, d),
            const(d, HP_COLS),
            const(d, KV_RANK),
            const(1, Q_RANK),
            const(1, KV_RANK),
            const(Q_RANK, A_HEADS * A_HEAD_DIM),
            const(A_HEADS, A_HEAD_DIM, KV_RANK),
            const(Q_RANK, IDX_HEADS * IDX_DIM),
        ],
        out_specs=[
            pl.BlockSpec((1, A_HEADS, tm, KV_RANK), lambda i, j: (i, 0, j, 0)),
            pl.BlockSpec((1, IDX_HEADS, tm, 3 * IDX_DIM), lambda i, j: (i, 0, j, 0)),
            pl.BlockSpec((1, tm, 3 * IDX_DIM), lambda i, j: (i, j, 0)),
            pl.BlockSpec((1, tm, IDX_HEADS), lambda i, j: (i, j, 0)),
            pl.BlockSpec((1, tm, KV_RANK), lambda i, j: (i, j, 0)),
        ],
        out_shape=[
            jax.ShapeDtypeStruct((b, A_HEADS, t, KV_RANK), BF16),
            jax.ShapeDtypeStruct((b, IDX_HEADS, t, 3 * IDX_DIM), BF16),
            jax.ShapeDtypeStruct((b, t, 3 * IDX_DIM), BF16),
            jax.ShapeDtypeStruct((b, t, IDX_HEADS), F32),
            jax.ShapeDtypeStruct((b, t, KV_RANK), BF16),
        ],
        compiler_params=_cparams(("arbitrary", "arbitrary")),
        name="dsa_pre",
    )(x, mod, g_mix.reshape(1, d), w_hp, w_kv, g_cq.reshape(1, Q_RANK), g_ckv.reshape(1, KV_RANK),
      w_uq.astype(BF16), w_uk.astype(BF16), w_qi)


def _bucket_starts():
    exact = NUM_BUCKETS // 2
    starts = list(range(1, exact + 1))
    d = exact
    for m in range(exact + 1, NUM_BUCKETS):
        while exact + int(math.log(d / exact) / math.log(MAX_DISTANCE / exact) * (NUM_BUCKETS - exact)) < m:
            d += 1
        starts.append(d)
    return tuple(starts)


BUCKET_STARTS = _bucket_starts()
FAR_DISTANCE = BUCKET_STARTS[-1]


def _bias_kernel(rb_ref, o_ref):
    hh = pl.program_id(0)
    row = lax.broadcasted_iota(I32, (ATT_TILE, ATT_TILE), 0)
    col = lax.broadcasted_iota(I32, (ATT_TILE, ATT_TILE), 1)
    far = rb_ref[NUM_BUCKETS - 1, hh]
    for kind, offset in ((0, ATT_TILE), (1, 0)):
        dist = jnp.maximum(row - col + offset, 0)
        acc = jnp.full((ATT_TILE, ATT_TILE), 0.0, F32) + far
        for m in range(NUM_BUCKETS - 2, -1, -1):
            acc = jnp.where(dist < BUCKET_STARTS[m], rb_ref[m, hh], acc)
        o_ref[kind, 0] = acc - far


def _bias_tiles(rel_bias):
    assert ATT_TILE >= FAR_DISTANCE
    return pl.pallas_call(
        _bias_kernel,
        grid=(A_HEADS,),
        in_specs=[pl.BlockSpec(memory_space=pltpu.SMEM)],
        out_specs=pl.BlockSpec((2, 1, ATT_TILE, ATT_TILE), lambda h: (0, h, 0, 0)),
        out_shape=jax.ShapeDtypeStruct((2, A_HEADS, ATT_TILE, ATT_TILE), F32),
        compiler_params=_cparams(("arbitrary",)),
        name="t5_bias_tiles",
    )(rel_bias)


def _dsa_attn_kernel(qi3_ref, ki3_ref, widx_ref, qlat_ref, ckv_ref, bias_ref, o_ref,
                     s_scr, m_scr, l_scr, acc_scr, *, n_sel, idx_bits):
    qi = pl.program_id(1)
    n_chunks = qi + 1
    tq = ATT_TILE
    rows = A_HEADS * tq
    q_pos = qi * tq + lax.broadcasted_iota(I32, (tq, 1), 0)
    lane = lax.broadcasted_iota(I32, (1, tq), 1)

    w_all = widx_ref[0]

    def score_chunk(c, carry):
        k3 = ki3_ref[0, pl.ds(pl.multiple_of(c * tq, tq), tq), :]
        sh = _dot_nt(qi3_ref[0].reshape(rows, 3 * IDX_DIM), k3).reshape(IDX_HEADS, tq, tq)
        sc = jnp.zeros((tq, tq), F32)
        for hh in range(IDX_HEADS):
            sc = sc + w_all[:, hh:hh + 1] * jnp.maximum(sh[hh], 0.0)
        k_pos = c * tq + lane
        s_scr[c] = jnp.where(k_pos <= q_pos, sc, NEG_INF)
        return carry

    lax.fori_loop(0, n_chunks, score_chunk, 0)

    def count(pred):
        def body(c, acc):
            ind = jnp.where(pred(s_scr[c], c * tq + lane), 1.0, 0.0)
            return acc + (ind[:, :LANES] + ind[:, LANES:])
        acc = lax.fori_loop(0, n_chunks, body, jnp.zeros((tq, LANES), F32))
        return jnp.sum(acc, axis=1, keepdims=True)

    def key_to_float(key):
        bits = jnp.where(key < 0, key ^ I32(-2 ** 31), ~key)
        return lax.bitcast_convert_type(bits, F32)

    def tau_bit(i, key):
        cand = key | lax.shift_left(I32(1), I32(31) - i)
        cand_f = key_to_float(cand)
        cnt = count(lambda s, kp: s >= cand_f)
        return jnp.where(cnt >= n_sel, cand, key)

    key = lax.fori_loop(0, 32, tau_bit, jnp.zeros((tq, 1), I32))
    fewer = (q_pos + 1) < n_sel
    tau = jnp.where(fewer, NEG_INF, key_to_float(key))

    n_gt = count(lambda s, kp: s > tau)
    need = jnp.where(fewer, 4.0 * ki3_ref.shape[1], n_sel - n_gt)

    def tie_bit(i, x):
        cand = x | lax.shift_left(I32(1), I32(idx_bits - 1) - i)
        cnt = count(lambda s, kp: (s == tau) & (kp < cand))
        return jnp.where(cnt < need, cand, x)

    last_tie = lax.fori_loop(0, idx_bits, tie_bit, jnp.zeros((tq, 1), I32))

    def mask_chunk(c, carry):
        s = s_scr[c]
        k_pos = c * tq + lane
        sel = (k_pos <= q_pos) & ((s > tau) | ((s == tau) & (k_pos <= last_tie)))
        s_scr[c] = jnp.where(sel, 0.0, NEG_INF)
        return carry

    lax.fori_loop(0, n_chunks, mask_chunk, 0)

    m_scr[...] = jnp.full((rows, 1), NEG_INF, F32)
    l_scr[...] = jnp.zeros((rows, 1), F32)
    acc_scr[...] = jnp.zeros((rows, KV_RANK), F32)

    def attend(c, bias_kind):
        kv = ckv_ref[0, pl.ds(pl.multiple_of(c * tq, tq), tq), :]
        logit = _dot_nt(qlat_ref[0].reshape(rows, KV_RANK), kv).reshape(A_HEADS, tq, tq)
        logit = logit + s_scr[c][None]
        if bias_kind is not None:
            logit = logit + bias_ref[bias_kind]
        logit = logit.reshape(rows, tq)
        m_old = m_scr[...]
        m_new = jnp.maximum(m_old, jnp.max(logit, axis=1, keepdims=True))
        m_safe = jnp.where(m_new == NEG_INF, 0.0, m_new)
        alpha = jnp.exp(m_old - m_safe)
        p = jnp.exp(logit - m_safe)
        l_scr[...] = alpha * l_scr[...] + jnp.sum(p, axis=1, keepdims=True)
        acc_scr[...] = alpha * acc_scr[...] + _dot(p.astype(BF16), kv)
        m_scr[...] = m_new

    def far_chunk(c, carry):
        attend(c, None)
        return carry

    lax.fori_loop(0, qi - 1, far_chunk, 0)

    @pl.when(qi >= 1)
    def _():
        attend(qi - 1, 0)

    attend(qi, 1)
    out = acc_scr[...] / l_scr[...]
    o_ref[0] = out.reshape(A_HEADS, tq, KV_RANK).astype(BF16)


def _dsa_attn(qi3, ki3, widx, qlat, ckv, bias):
    b, t, _ = ckv.shape
    tq = ATT_TILE
    n_sel = min(TOPK_MAX, t // 4)
    idx_bits = max(1, (t - 1).bit_length())
    rows = A_HEADS * tq
    kernel = functools.partial(_dsa_attn_kernel, n_sel=n_sel, idx_bits=idx_bits)
    return pl.pallas_call(
        kernel,
        grid=(b, t // tq),
        in_specs=[
            pl.BlockSpec((1, IDX_HEADS, tq, 3 * IDX_DIM), lambda i, j: (i, 0, j, 0)),
            pl.BlockSpec((1, t, 3 * IDX_DIM), lambda i, j: (i, 0, 0)),
            pl.BlockSpec((1, tq, IDX_HEADS), lambda i, j: (i, j, 0)),
            pl.BlockSpec((1, A_HEADS, tq, KV_RANK), lambda i, j: (i, 0, j, 0)),
            pl.BlockSpec((1, t, KV_RANK), lambda i, j: (i, 0, 0)),
            pl.BlockSpec((2, A_HEADS, tq, tq), lambda i, j: (0, 0, 0, 0)),
        ],
        out_specs=pl.BlockSpec((1, A_HEADS, tq, KV_RANK), lambda i, j: (i, 0, j, 0)),
        out_shape=jax.ShapeDtypeStruct((b, A_HEADS, t, KV_RANK), BF16),
        scratch_shapes=[
            pltpu.VMEM((t // tq, tq, tq), F32),
            pltpu.VMEM((rows, 1), F32),
            pltpu.VMEM((rows, 1), F32),
            pltpu.VMEM((rows, KV_RANK), F32),
        ],
        compiler_params=_cparams(("arbitrary", "arbitrary")),
        name="dsa_attn",
    )(qi3, ki3, widx, qlat, ckv, bias)


def _dsa_post_kernel(olat_ref, wuv_ref, wo_ref, x_ref, mod_ref, o_ref):
    heads = [_dot(olat_ref[0, hh], wuv_ref[hh]).astype(BF16) for hh in range(A_HEADS)]
    y = _dot(jnp.concatenate(heads, axis=1), wo_ref[...])
    o_ref[0] = x_ref[0] + mod_ref[0][2:3] * y


def _dsa_post(olat, w_uv, w_o, x, mod):
    b, t, d = x.shape
    tm = min(ROW_TILE, t)
    return pl.pallas_call(
        _dsa_post_kernel,
        grid=(b, t // tm),
        in_specs=[
            pl.BlockSpec((1, A_HEADS, tm, KV_RANK), lambda i, j: (i, 0, j, 0)),
            pl.BlockSpec((A_HEADS, KV_RANK, A_V_DIM), lambda i, j: (0, 0, 0)),
            pl.BlockSpec((A_HEADS * A_V_DIM, d), lambda i, j: (0, 0)),
            pl.BlockSpec((1, tm, d), lambda i, j: (i, j, 0)),
            pl.BlockSpec((1, 6, d), lambda i, j: (i, 0, 0)),
        ],
        out_specs=pl.BlockSpec((1, tm, d), lambda i, j: (i, j, 0)),
        out_shape=jax.ShapeDtypeStruct((b, t, d), F32),
        compiler_params=_cparams(("arbitrary", "arbitrary")),
        name="dsa_post",
    )(olat, w_uv.astype(BF16), w_o.astype(BF16), x, mod)


CONV_HALO = 8


def _ffn_kernel(x_ref, mod_ref, g_ref, wu_ref, wv_ref, cw_ref, cb_ref, wout_ref, gfin_ref,
                o_ref, h_scr, acc_scr, u_scr, carry_scr, *, final_norm):
    ti = pl.program_id(1)
    f = pl.program_id(2)
    tm = x_ref.shape[1]
    mod = mod_ref[0]

    @pl.when(f == 0)
    def _():
        h_scr[...] = _norm_mod(x_ref[0], g_ref[...], mod[4:5], mod[3:4]).astype(BF16)
        acc_scr[...] = jnp.zeros(acc_scr.shape, F32)

    @pl.when(ti == 0)
    def _():
        carry_scr[f] = jnp.zeros(carry_scr.shape[1:], F32)

    h = h_scr[...]
    u = _dot(h, wu_ref[...])
    v = _dot(h, wv_ref[...])
    u_scr[0:CONV_HALO] = carry_scr[f]
    u_scr[CONV_HALO:CONV_HALO + tm] = u
    carry_scr[f] = u[tm - CONV_HALO:tm]
    cw = cw_ref[...]
    u1 = u_scr[CONV_HALO - 1:CONV_HALO - 1 + tm]
    u2 = u_scr[CONV_HALO - 2:CONV_HALO - 2 + tm]
    uc = cw[0:1] * u2 + cw[1:2] * u1 + cw[2:3] * u + cb_ref[...]
    act = 0.5 * uc * (1.0 + lax.erf(uc * (2.0 ** -0.5)))
    acc_scr[...] += _dot((act * v).astype(BF16), wout_ref[...])

    @pl.when(f == pl.num_programs(2) - 1)
    def _():
        y = x_ref[0] + mod[5:6] * acc_scr[...]
        if final_norm:
            y = _rms(y) * gfin_ref[...]
        o_ref[0] = y


def _conv_ffn(x, mod, g_ffn, w_in, conv_w, conv_b, w_out, g_final, final_norm):
    b, t, d = x.shape
    tm = min(ROW_TILE, t)
    fc = D_FF // FF_SPLIT
    assert fc % LANES == 0
    w_in_b = w_in.astype(BF16)
    kernel = functools.partial(_ffn_kernel, final_norm=final_norm)
    return pl.pallas_call(
        kernel,
        grid=(b, t // tm, FF_SPLIT),
        in_specs=[
            pl.BlockSpec((1, tm, d), lambda i, j, f: (i, j, 0)),
            pl.BlockSpec((1, 6, d), lambda i, j, f: (i, 0, 0)),
            pl.BlockSpec((1, d), lambda i, j, f: (0, 0)),
            pl.BlockSpec((d, fc), lambda i, j, f: (0, f)),
            pl.BlockSpec((d, fc), lambda i, j, f: (0, FF_SPLIT + f)),
            pl.BlockSpec((3, fc), lambda i, j, f: (0, f)),
            pl.BlockSpec((1, fc), lambda i, j, f: (0, f)),
            pl.BlockSpec((fc, d), lambda i, j, f: (f, 0)),
            pl.BlockSpec((1, d), lambda i, j, f: (0, 0)),
        ],
        out_specs=pl.BlockSpec((1, tm, d), lambda i, j, f: (i, j, 0)),
        out_shape=jax.ShapeDtypeStruct((b, t, d), F32),
        scratch_shapes=[
            pltpu.VMEM((tm, d), BF16),
            pltpu.VMEM((tm, d), F32),
            pltpu.VMEM((tm + CONV_HALO, fc), F32),
            pltpu.VMEM((FF_SPLIT, CONV_HALO, fc), F32),
        ],
        compiler_params=_cparams(("arbitrary", "arbitrary", "arbitrary")),
        name="conv_ffn_final" if final_norm else "conv_ffn",
    )(x, mod, g_ffn.reshape(1, d), w_in_b, w_in_b, conv_w, conv_b.reshape(1, D_FF),
      w_out.astype(BF16), g_final.reshape(1, d))


def _gla_pre_kernel(x_ref, mod_ref, g_ref, wqkr_ref, wv_ref, wvt_ref, wg_ref, wg2_ref, bg_ref,
                    q_ref, k_ref, r_ref, v_ref, vt_ref, la_ref):
    mod = mod_ref[0]
    h = _norm_mod(x_ref[0], g_ref[...], mod[1:2], mod[0:1])
    hb = h.astype(BF16)
    qkr = _dot(hb, wqkr_ref[...])
    q_ref[0] = (qkr[:, :G_KD] * (G_DK ** -0.5)).astype(BF16)
    k_ref[0] = qkr[:, G_KD:2 * G_KD].astype(BF16)
    r_ref[0] = qkr[:, 2 * G_KD:].astype(BF16)
    v_ref[0] = _dot(hb, wv_ref[...]).astype(BF16)
    vt_ref[0] = _dot_nt(wvt_ref[...], hb).astype(BF16)
    g_lr = _dot_hp(h, wg_ref[...])
    gate = _dot_hp(g_lr, wg2_ref[...]) + bg_ref[...]
    log_sig = jnp.minimum(gate, 0.0) - jnp.log1p(jnp.exp(-jnp.abs(gate)))
    la_ref[0] = log_sig / GATE_TAU


def _gla_pre(x, mod, g_mix, w_in, w_g2, b_g):
    b, t, d = x.shape
    tm = min(ROW_TILE, t)
    w_q, w_k, w_v, w_r, w_g = jnp.split(w_in, [G_KD, 2 * G_KD, 2 * G_KD + G_VD, 2 * G_KD + 2 * G_VD], axis=1)
    w_qkr = jnp.concatenate([w_q, w_k, w_r], axis=1).astype(BF16)
    w_g_pad = jnp.concatenate([w_g, jnp.zeros((d, LANES - G_RANK), F32)], axis=1)
    w_g2_pad = jnp.concatenate([w_g2, jnp.zeros((LANES - G_RANK, G_KD), F32)], axis=0)
    const = lambda *shape: pl.BlockSpec(shape, lambda i, j: (0,) * len(shape))
    row = lambda width: pl.BlockSpec((1, tm, width), lambda i, j: (i, j, 0))
    return pl.pallas_call(
        _gla_pre_kernel,
        grid=(b, t // tm),
        in_specs=[
            row(d),
            pl.BlockSpec((1, 6, d), lambda i, j: (i, 0, 0)),
            const(1, d),
            const(d, 2 * G_KD + G_VD),
            const(d, G_VD),
            const(G_VD, d),
            const(d, LANES),
            const(LANES, G_KD),
            const(1, G_KD),
        ],
        out_specs=[row(G_KD), row(G_KD), row(G_VD), row(G_VD),
                   pl.BlockSpec((1, G_VD, tm), lambda i, j: (i, 0, j)), row(G_KD)],
        out_shape=[
            jax.ShapeDtypeStruct((b, t, G_KD), BF16),
            jax.ShapeDtypeStruct((b, t, G_KD), BF16),
            jax.ShapeDtypeStruct((b, t, G_VD), BF16),
            jax.ShapeDtypeStruct((b, t, G_VD), BF16),
            jax.ShapeDtypeStruct((b, G_VD, t), BF16),
            jax.ShapeDtypeStruct((b, t, G_KD), F32),
        ],
        compiler_params=_cparams(("arbitrary", "arbitrary")),
        name="gla_pre",
    )(x, mod, g_mix.reshape(1, d), w_qkr, w_v.astype(BF16), w_v.T.astype(BF16), w_g_pad, w_g2_pad,
      b_g.reshape(1, G_KD))


GLA_LEVELS = tuple(1 << p for p in range(GLA_CHUNK.bit_length() - 1))


def _gla_kernel(q_ref, k_ref, g_ref, v_ref, vt_ref, o_ref, st_scr):
    n_ch = q_ref.shape[1] // GLA_CHUNK
    cl = GLA_CHUNK

    @pl.when(pl.program_id(2) == 0)
    def _():
        st_scr[...] = jnp.zeros(st_scr.shape, F32)

    def lanes(ref):
        x = ref[0]
        return jnp.concatenate([x[c * cl:(c + 1) * cl] for c in range(n_ch)], axis=1)

    q = lanes(q_ref).astype(F32)
    k = lanes(k_ref).astype(F32)
    g1, g2, g3 = _split3(lanes(g_ref))

    r = lax.broadcasted_iota(I32, ((len(GLA_LEVELS) + 1) * cl, cl), 0)
    cidx = lax.broadcasted_iota(I32, ((len(GLA_LEVELS) + 1) * cl, cl), 1)
    i_loc = r & (cl - 1)
    lvl = lax.shift_right_logical(r, I32(cl.bit_length() - 1))
    half = lax.shift_left(I32(1), lvl)
    ref_row = jnp.where(lvl < len(GLA_LEVELS), (i_loc & ~(2 * half - 1)) | (half - 1), -1)
    sel = (jnp.where(cidx <= i_loc, 1.0, 0.0) - jnp.where(cidx <= ref_row, 1.0, 0.0)).astype(BF16)
    d_all = _dot(sel, g1) + (_dot(sel, g2) + _dot(sel, g3))

    row = lax.broadcasted_iota(I32, (cl, 1), 0)
    xor = lax.broadcasted_iota(I32, (cl, cl), 0) ^ lax.broadcasted_iota(I32, (cl, cl), 1)

    a = [jnp.where(xor == 0, _dot_nt(q[:, c * G_DK:(c + 1) * G_DK].astype(BF16),
                                     k[:, c * G_DK:(c + 1) * G_DK].astype(BF16)), 0.0)
         for c in range(n_ch)]
    for li, s in enumerate(GLA_LEVELS):
        dl = d_all[li * cl:(li + 1) * cl]
        hi = (row & s) != 0
        e = jnp.exp(jnp.where(hi, dl, -dl))
        qt = jnp.where(hi, q * e, 0.0).astype(BF16)
        kt = jnp.where(hi, 0.0, k * e).astype(BF16)
        for c in range(n_ch):
            p = _dot_nt(qt[:, c * G_DK:(c + 1) * G_DK], kt[:, c * G_DK:(c + 1) * G_DK])
            a[c] = a[c] + jnp.where(xor < 2 * s, p, 0.0)

    b = d_all[len(GLA_LEVELS) * cl:]
    b_last = b[cl - 1:cl]
    q_hat = (q * jnp.exp(b)).astype(BF16)
    k_hat = (k * jnp.exp(b_last - b)).astype(BF16)
    decay = jnp.exp(b_last)

    st = st_scr[...]
    v_all = v_ref[0]
    vt_all = vt_ref[0]
    outs = []
    for c in range(n_ch):
        sl = slice(c * G_DK, (c + 1) * G_DK)
        o_inter = _dot_nt(q_hat[:, sl], st.astype(BF16))
        o_intra = _dot(a[c].astype(BF16), v_all[c * cl:(c + 1) * cl])
        outs.append(o_inter + o_intra)
        st = st * decay[:, sl] + _dot(vt_all[:, c * cl:(c + 1) * cl], k_hat[:, sl])
    st_scr[...] = st
    o_ref[0] = jnp.concatenate(outs, axis=0).astype(BF16)


def _gla_core(q, k, la, v, vt):
    b, t, _ = q.shape
    tm = min(GLA_TILE, t)
    assert G_DK == LANES and tm % GLA_CHUNK == 0
    return pl.pallas_call(
        _gla_kernel,
        grid=(b, G_HEADS, t // tm),
        in_specs=[
            pl.BlockSpec((1, tm, G_DK), lambda i, h, j: (i, j, h)),
            pl.BlockSpec((1, tm, G_DK), lambda i, h, j: (i, j, h)),
            pl.BlockSpec((1, tm, G_DK), lambda i, h, j: (i, j, h)),
            pl.BlockSpec((1, tm, G_DV), lambda i, h, j: (i, j, h)),
            pl.BlockSpec((1, G_DV, tm), lambda i, h, j: (i, h, j)),
        ],
        out_specs=pl.BlockSpec((1, tm, G_DV), lambda i, h, j: (i, j, h)),
        out_shape=jax.ShapeDtypeStruct((b, t, G_VD), BF16),
        scratch_shapes=[pltpu.VMEM((G_DV, G_DK), F32)],
        compiler_params=_cparams(("arbitrary", "arbitrary", "arbitrary")),
        name="gla_core",
    )(q, k, la, v, vt)


def _gla_post_kernel(o_ref, r_ref, gn_ref, wo_ref, x_ref, mod_ref, out_ref):
    o = o_ref[0].astype(F32)
    gn = gn_ref[...]
    heads = [_rms(o[:, hh * G_DV:(hh + 1) * G_DV]) * gn[:, hh * G_DV:(hh + 1) * G_DV]
             for hh in range(G_HEADS)]
    r = r_ref[0].astype(F32)
    z = jnp.concatenate(heads, axis=1) * (r * jax.nn.sigmoid(r))
    out_ref[0] = x_ref[0] + mod_ref[0][2:3] * _dot(z.astype(BF16), wo_ref[...])


def _gla_post(o, r, g_norm, w_o, x, mod):
    b, t, d = x.shape
    tm = min(ROW_TILE, t)
    row = lambda width: pl.BlockSpec((1, tm, width), lambda i, j: (i, j, 0))
    return pl.pallas_call(
        _gla_post_kernel,
        grid=(b, t // tm),
        in_specs=[row(G_VD), row(G_VD),
                  pl.BlockSpec((1, G_VD), lambda i, j: (0, 0)),
                  pl.BlockSpec((G_VD, d), lambda i, j: (0, 0)),
                  row(d),
                  pl.BlockSpec((1, 6, d), lambda i, j: (i, 0, 0))],
        out_specs=row(d),
        out_shape=jax.ShapeDtypeStruct((b, t, d), F32),
        compiler_params=_cparams(("arbitrary", "arbitrary")),
        name="gla_post",
    )(o, r, g_norm.reshape(1, G_VD), w_o.astype(BF16), x, mod)


def kernel(x, c, rel_bias, a_w_in, a_g_cq, a_g_ckv, a_w_uq, a_w_uk, a_w_uv, a_w_qi, a_w_o, b_w_in, b_w_g2, b_b_g, b_g_norm, b_w_o, ada_w, ada_b, g_mix, g_ffn, f_w_in, f_conv_w, f_conv_b, f_w_out, g_final):
    b, t, d = x.shape
    depth = ada_w.shape[0]
    assert d == D_MODEL and depth == 2 and t % ROW_TILE == 0
    mod = _modulation(c, ada_w, ada_b).reshape(depth, b, 6, d)

    qlat, qi3, ki3, widx, ckv = _dsa_pre(x, mod[0], g_mix[0], a_w_in[0], a_g_cq[0], a_g_ckv[0],
                                         a_w_uq[0], a_w_uk[0], a_w_qi[0])
    olat = _dsa_attn(qi3, ki3, widx, qlat, ckv, _bias_tiles(rel_bias))
    x = _dsa_post(olat, a_w_uv[0], a_w_o[0], x, mod[0])
    x = _conv_ffn(x, mod[0], g_ffn[0], f_w_in[0], f_conv_w[0], f_conv_b[0], f_w_out[0], g_final, False)

    q, k, r, v, vt, la = _gla_pre(x, mod[1], g_mix[1], b_w_in[0], b_w_g2[0], b_b_g[0])
    o = _gla_core(q, k, la, v, vt)
    x = _gla_post(o, r, b_g_norm[0], b_w_o[0], x, mod[1])
    return _conv_ffn(x, mod[1], g_ffn[1], f_w_in[1], f_conv_w[1], f_conv_b[1], f_w_out[1], g_final, True)
```

```python
import functools
import math

import numpy as np
import jax
import jax.numpy as jnp
from jax import lax
from jax.experimental import pallas as pl
from jax.experimental.pallas import tpu as pltpu

F32 = jnp.float32
BF16 = jnp.bfloat16
I32 = jnp.int32

D_MODEL = 1024
A_HEADS = 8
A_HEAD_DIM = 128
A_V_DIM = 128
Q_RANK = 256
KV_RANK = 256
IDX_HEADS = 8
IDX_DIM = 64
TOPK_MAX = 256
NUM_BUCKETS = 32
MAX_DISTANCE = 128
G_HEADS = 4
G_KD = 512
G_VD = 1024
G_DK = G_KD // G_HEADS
G_DV = G_VD // G_HEADS
G_RANK = 16
GATE_TAU = 16.0
D_FF = 2816
EPS = 1e-6

LANES = 128
VMEM_LIMIT_BYTES = 56 * 1024 * 1024

ROW_TILE = 512
ATT_TILE = 256
GLA_TILE = 512
GLA_CHUNK = 128
FF_SPLIT = 2
NEG_INF = float("-inf")


def _cparams(sem):
    return pltpu.CompilerParams(dimension_semantics=sem, vmem_limit_bytes=VMEM_LIMIT_BYTES)


def _dot(a, b):
    return lax.dot_general(a, b, (((1,), (0,)), ((), ())), preferred_element_type=F32)


def _dot_nt(a, b):
    return lax.dot_general(a, b, (((1,), (1,)), ((), ())), preferred_element_type=F32)


def _split2(a):
    hi = a.astype(BF16)
    lo = (a - hi.astype(F32)).astype(BF16)
    return hi, lo


def _split3(a):
    p1 = a.astype(BF16)
    r1 = a - p1.astype(F32)
    p2 = r1.astype(BF16)
    p3 = (r1 - p2.astype(F32)).astype(BF16)
    return p1, p2, p3


def _dot_hp(a, b):
    ah, al = _split2(a)
    bh, bl = _split2(b)
    return _dot(ah, bh) + (_dot(ah, bl) + _dot(al, bh))


def _dot_hp_nt(a, b):
    ah, al = _split2(a)
    bh, bl = _split2(b)
    return _dot_nt(ah, bh) + (_dot_nt(ah, bl) + _dot_nt(al, bh))


def _hi_lo_concat(a, order):
    hi = a.astype(BF16).astype(F32)
    parts = (hi, a - hi)
    return jnp.concatenate([parts[o] for o in order], axis=1).astype(BF16)


def _rms(x):
    return x * lax.rsqrt(jnp.mean(x * x, axis=-1, keepdims=True) + EPS)


def _norm_mod(x, g, sc, sh):
    return (_rms(x) * g) * (1.0 + sc) + sh


def _mod_kernel(c_ref, w_ref, b_ref, o_ref):
    c = c_ref[...]
    cond = c * jax.nn.sigmoid(c)
    o_ref[0] = _dot_hp(cond, w_ref[0]) + b_ref[0]


def _modulation(c, ada_w, ada_b):
    depth, d, six_d = ada_w.shape
    b = c.shape[0]
    n_col = six_d // d
    return pl.pallas_call(
        _mod_kernel,
        grid=(depth, n_col),
        in_specs=[
            pl.BlockSpec((b, d), lambda l, j: (0, 0)),
            pl.BlockSpec((1, d, d), lambda l, j: (l, 0, j)),
            pl.BlockSpec((1, 1, d), lambda l, j: (l, 0, j)),
        ],
        out_specs=pl.BlockSpec((1, b, d), lambda l, j: (l, 0, j)),
        out_shape=jax.ShapeDtypeStruct((depth, b, six_d), F32),
        compiler_params=_cparams(("arbitrary", "arbitrary")),
        name="adaln_mod",
    )(c, ada_w, ada_b.reshape(depth, 1, six_d))


def _rms_cols(xt):
    return xt * lax.rsqrt(jnp.mean(xt * xt, axis=0, keepdims=True) + EPS)


def _dsa_pre_kernel(x_ref, mod_ref, g_ref, wk_ref, wkv_ref, wkvt_ref, wqt_ref, gcq_ref, gckv_ref,
                    gckvc_ref, wuqt_ref, wukt_ref, wqit_ref,
                    qlatt_ref, qi3t_ref, ki3_ref, widxt_ref, ckv_ref, ckvt_ref):
    mod = mod_ref[0]
    h = _norm_mod(x_ref[0], g_ref[...], mod[1:2], mod[0:1])
    hb = h.astype(BF16)

    k_idx = _dot_hp(h, wk_ref[...])[:, :IDX_DIM]
    ki3_ref[0] = _hi_lo_concat(k_idx, (0, 1, 0))
    ckv_ref[0] = (_rms(_dot(hb, wkv_ref[...])) * gckv_ref[...]).astype(BF16)
    c_kvt = (_rms_cols(_dot_nt(wkvt_ref[...], hb)) * gckvc_ref[...]).astype(BF16)
    for j in range(ckvt_ref.shape[1]):
        ckvt_ref[0, j] = c_kvt[:, j * ATT_TILE:(j + 1) * ATT_TILE]

    hpt = _dot_hp_nt(wqt_ref[...], h)
    widxt_ref[0] = hpt[Q_RANK:] * (IDX_HEADS ** -0.5 * IDX_DIM ** -0.5)
    c_qt = _rms_cols(hpt[:Q_RANK]) * gcq_ref[...]
    qt = _dot(wuqt_ref[...], c_qt.astype(BF16))
    for hh in range(A_HEADS):
        qh = qt[hh * A_HEAD_DIM:(hh + 1) * A_HEAD_DIM].astype(BF16)
        qlatt_ref[0, hh] = (_dot(wukt_ref[hh], qh) * (A_HEAD_DIM ** -0.5)).astype(BF16)
    q_idxt = _dot_hp(wqit_ref[...], c_qt)
    for hh in range(IDX_HEADS):
        piece = q_idxt[hh * IDX_DIM:(hh + 1) * IDX_DIM]
        hi = piece.astype(BF16).astype(F32)
        qi3t_ref[0, hh] = jnp.concatenate([hi, hi, piece - hi], axis=0).astype(BF16)


def _dsa_pre(x, mod, g_mix, w_in, g_cq, g_ckv, w_uq, w_uk, w_qi):
    b, t, d = x.shape
    tm = min(ROW_TILE, t)
    w_cq, w_ckv, w_ki, w_wi = jnp.split(w_in, [Q_RANK, Q_RANK + KV_RANK, Q_RANK + KV_RANK + IDX_DIM], axis=1)
    w_k = jnp.concatenate([w_ki, jnp.zeros((d, LANES - IDX_DIM), F32)], axis=1)
    w_qt = jnp.concatenate([w_cq, w_wi], axis=1).T
    const = lambda *shape: pl.BlockSpec(shape, lambda i, j: (0,) * len(shape))
    return pl.pallas_call(
        _dsa_pre_kernel,
        grid=(b, t // tm),
        in_specs=[
            pl.BlockSpec((1, tm, d), lambda i, j: (i, j, 0)),
            pl.BlockSpec((1, 6, d), lambda i, j: (i, 0, 0)),
            const(1, d),
            const(d, LANES),
            const(d, KV_RANK),
            const(KV_RANK, d),
            const(Q_RANK + IDX_HEADS, d),
            const(Q_RANK, 1),
            const(1, KV_RANK),
            const(KV_RANK, 1),
            const(A_HEADS * A_HEAD_DIM, Q_RANK),
            const(A_HEADS, KV_RANK, A_HEAD_DIM),
            const(IDX_HEADS * IDX_DIM, Q_RANK),
        ],
        out_specs=[
            pl.BlockSpec((1, A_HEADS, KV_RANK, tm), lambda i, j: (i, 0, 0, j)),
            pl.BlockSpec((1, IDX_HEADS, 3 * IDX_DIM, tm), lambda i, j: (i, 0, 0, j)),
            pl.BlockSpec((1, tm, 3 * IDX_DIM), lambda i, j: (i, j, 0)),
            pl.BlockSpec((1, IDX_HEADS, tm), lambda i, j: (i, 0, j)),
            pl.BlockSpec((1, tm, KV_RANK), lambda i, j: (i, j, 0)),
            pl.BlockSpec((1, tm // ATT_TILE, KV_RANK, ATT_TILE), lambda i, j: (i, j, 0, 0)),
        ],
        out_shape=[
            jax.ShapeDtypeStruct((b, A_HEADS, KV_RANK, t), BF16),
            jax.ShapeDtypeStruct((b, IDX_HEADS, 3 * IDX_DIM, t), BF16),
            jax.ShapeDtypeStruct((b, t, 3 * IDX_DIM), BF16),
            jax.ShapeDtypeStruct((b, IDX_HEADS, t), F32),
            jax.ShapeDtypeStruct((b, t, KV_RANK), BF16),
            jax.ShapeDtypeStruct((b, t // ATT_TILE, KV_RANK, ATT_TILE), BF16),
        ],
        compiler_params=_cparams(("arbitrary", "arbitrary")),
        name="dsa_pre",
    )(x, mod, g_mix.reshape(1, d), w_k, w_ckv.astype(BF16), w_ckv.T.astype(BF16), w_qt,
      g_cq.reshape(Q_RANK, 1), g_ckv.reshape(1, KV_RANK), g_ckv.reshape(KV_RANK, 1),
      w_uq.T.astype(BF16), jnp.swapaxes(w_uk, 1, 2).astype(BF16), w_qi.T)


def _bucket_starts():
    exact = NUM_BUCKETS // 2
    starts = list(range(1, exact + 1))
    d = exact
    for m in range(exact + 1, NUM_BUCKETS):
        while exact + int(math.log(d / exact) / math.log(MAX_DISTANCE / exact) * (NUM_BUCKETS - exact)) < m:
            d += 1
        starts.append(d)
    return tuple(starts)


BUCKET_STARTS = _bucket_starts()
FAR_DISTANCE = BUCKET_STARTS[-1]


def _bias_kernel(rb_ref, o_ref):
    hh = pl.program_id(0)
    key = lax.broadcasted_iota(I32, (ATT_TILE, ATT_TILE), 0)
    qry = lax.broadcasted_iota(I32, (ATT_TILE, ATT_TILE), 1)
    far = rb_ref[NUM_BUCKETS - 1, hh]
    for kind, offset in ((0, ATT_TILE), (1, 0)):
        dist = jnp.maximum(qry - key + offset, 0)
        acc = jnp.full((ATT_TILE, ATT_TILE), 0.0, F32) + far
        for m in range(NUM_BUCKETS - 2, -1, -1):
            acc = jnp.where(dist < BUCKET_STARTS[m], rb_ref[m, hh], acc)
        o_ref[kind, 0] = acc - far


def _bias_tiles(rel_bias):
    assert ATT_TILE >= FAR_DISTANCE
    return pl.pallas_call(
        _bias_kernel,
        grid=(A_HEADS,),
        in_specs=[pl.BlockSpec(memory_space=pltpu.SMEM)],
        out_specs=pl.BlockSpec((2, 1, ATT_TILE, ATT_TILE), lambda h: (0, h, 0, 0)),
        out_shape=jax.ShapeDtypeStruct((2, A_HEADS, ATT_TILE, ATT_TILE), F32),
        compiler_params=_cparams(("arbitrary",)),
        name="t5_bias_tiles",
    )(rel_bias)


def _dsa_attn_kernel(qi3t_ref, ki3_ref, widxt_ref, qlatt_ref, ckv_ref, ckvt_ref, bias_ref, o_ref,
                     s_scr, m_scr, l_scr, acc_scr, *, n_sel, idx_bits):
    qi = pl.program_id(1)
    n_chunks = qi + 1
    tq = ATT_TILE
    q_pos = qi * tq + lax.broadcasted_iota(I32, (1, tq), 1)
    k_loc = lax.broadcasted_iota(I32, (tq, 1), 0)

    w_all = widxt_ref[0]

    def score_chunk(c, carry):
        k3 = ki3_ref[0, pl.ds(pl.multiple_of(c * tq, tq), tq), :]
        sc = jnp.zeros((tq, tq), F32)
        for hh in range(IDX_HEADS):
            sc = sc + w_all[hh:hh + 1, :] * jnp.maximum(_dot(k3, qi3t_ref[0, hh]), 0.0)
        s_scr[c] = jnp.where(c * tq + k_loc <= q_pos, sc, NEG_INF)
        return carry

    lax.fori_loop(0, n_chunks, score_chunk, 0)

    def count(pred):
        def body(c, acc):
            ind = jnp.where(pred(s_scr[c], c * tq + k_loc), 1.0, 0.0).astype(F32)
            return acc + jnp.sum(ind.reshape(tq // 8, 8, tq), axis=0)
        acc = lax.fori_loop(0, n_chunks, body, jnp.zeros((8, tq), F32))
        return jnp.sum(acc, axis=0, keepdims=True)

    def key_to_float(key):
        bits = jnp.where(key < 0, key ^ I32(-2 ** 31), ~key)
        return lax.bitcast_convert_type(bits, F32)

    def tau_bit(i, key):
        cand = key | lax.shift_left(I32(1), I32(31) - i)
        cand_f = key_to_float(cand)
        cnt = count(lambda s, kp: s >= cand_f)
        return jnp.where(cnt >= n_sel, cand, key)

    key = lax.fori_loop(0, 32, tau_bit, jnp.zeros((1, tq), I32))
    fewer = (q_pos + 1) < n_sel
    tau = jnp.where(fewer, NEG_INF, key_to_float(key))

    n_gt = count(lambda s, kp: s > tau)
    need = jnp.where(fewer, 4.0 * ki3_ref.shape[1], n_sel - n_gt)

    def tie_bit(i, x):
        cand = x | lax.shift_left(I32(1), I32(idx_bits - 1) - i)
        cnt = count(lambda s, kp: (s == tau) & (kp < cand))
        return jnp.where(cnt < need, cand, x)

    last_tie = lax.fori_loop(0, idx_bits, tie_bit, jnp.zeros((1, tq), I32))

    def mask_chunk(c, carry):
        s = s_scr[c]
        k_pos = c * tq + k_loc
        sel = (k_pos <= q_pos) & ((s > tau) | ((s == tau) & (k_pos <= last_tie)))
        s_scr[c] = jnp.where(sel, 0.0, NEG_INF)
        return carry

    lax.fori_loop(0, n_chunks, mask_chunk, 0)

    m_scr[...] = jnp.full(m_scr.shape, NEG_INF, F32)
    l_scr[...] = jnp.zeros(l_scr.shape, F32)
    acc_scr[...] = jnp.zeros(acc_scr.shape, F32)

    def attend(c, bias_kind):
        kv = ckv_ref[0, pl.ds(pl.multiple_of(c * tq, tq), tq), :]
        kvt = ckvt_ref[0, c]
        mask = s_scr[c]
        for hh in range(A_HEADS):
            logit = _dot(kv, qlatt_ref[0, hh]) + mask
            if bias_kind is not None:
                logit = logit + bias_ref[bias_kind, hh]
            m_old = m_scr[hh:hh + 1]
            m_new = jnp.maximum(m_old, jnp.max(logit, axis=0, keepdims=True))
            m_safe = jnp.where(m_new == NEG_INF, 0.0, m_new)
            alpha = jnp.exp(m_old - m_safe)
            p = jnp.exp(logit - m_safe)
            l_scr[hh:hh + 1] = alpha * l_scr[hh:hh + 1] + jnp.sum(p, axis=0, keepdims=True)
            acc_scr[hh] = alpha * acc_scr[hh] + _dot(kvt, p.astype(BF16))
            m_scr[hh:hh + 1] = m_new

    def far_chunk(c, carry):
        attend(c, None)
        return carry

    lax.fori_loop(0, qi - 1, far_chunk, 0)

    @pl.when(qi >= 1)
    def _():
        attend(qi - 1, 0)

    attend(qi, 1)
    for hh in range(A_HEADS):
        o_ref[0, hh] = (acc_scr[hh] / l_scr[hh:hh + 1]).astype(BF16)


def _dsa_attn(qi3t, ki3, widxt, qlatt, ckv, ckvt, bias):
    b, t, _ = ckv.shape
    tq = ATT_TILE
    n_sel = min(TOPK_MAX, t // 4)
    idx_bits = max(1, (t - 1).bit_length())
    kernel = functools.partial(_dsa_attn_kernel, n_sel=n_sel, idx_bits=idx_bits)
    return pl.pallas_call(
        kernel,
        grid=(b, t // tq),
        in_specs=[
            pl.BlockSpec((1, IDX_HEADS, 3 * IDX_DIM, tq), lambda i, j: (i, 0, 0, j)),
            pl.BlockSpec((1, t, 3 * IDX_DIM), lambda i, j: (i, 0, 0)),
            pl.BlockSpec((1, IDX_HEADS, tq), lambda i, j: (i, 0, j)),
            pl.BlockSpec((1, A_HEADS, KV_RANK, tq), lambda i, j: (i, 0, 0, j)),
            pl.BlockSpec((1, t, KV_RANK), lambda i, j: (i, 0, 0)),
            pl.BlockSpec((1, t // tq, KV_RANK, tq), lambda i, j: (i, 0, 0, 0)),
            pl.BlockSpec((2, A_HEADS, tq, tq), lambda i, j: (0, 0, 0, 0)),
        ],
        out_specs=pl.BlockSpec((1, A_HEADS, KV_RANK, tq), lambda i, j: (i, 0, 0, j)),
        out_shape=jax.ShapeDtypeStruct((b, A_HEADS, KV_RANK, t), BF16),
        scratch_shapes=[
            pltpu.VMEM((t // tq, tq, tq), F32),
            pltpu.VMEM((A_HEADS, tq), F32),
            pltpu.VMEM((A_HEADS, tq), F32),
            pltpu.VMEM((A_HEADS, KV_RANK, tq), F32),
        ],
        compiler_params=_cparams(("arbitrary", "arbitrary")),
        name="dsa_attn",
    )(qi3t, ki3, widxt, qlatt, ckv, ckvt, bias)


def _dsa_post_kernel(olatt_ref, wuvt_ref, wo_ref, x_ref, mod_ref, o_ref):
    heads = [_dot(wuvt_ref[hh], olatt_ref[0, hh]).astype(BF16) for hh in range(A_HEADS)]
    ot = jnp.concatenate(heads, axis=0)
    y = lax.dot_general(ot, wo_ref[...], (((0,), (0,)), ((), ())), preferred_element_type=F32)
    o_ref[0] = x_ref[0] + mod_ref[0][2:3] * y


def _dsa_post(olatt, w_uv, w_o, x, mod):
    b, t, d = x.shape
    tm = min(ROW_TILE, t)
    return pl.pallas_call(
        _dsa_post_kernel,
        grid=(b, t // tm),
        in_specs=[
            pl.BlockSpec((1, A_HEADS, KV_RANK, tm), lambda i, j: (i, 0, 0, j)),
            pl.BlockSpec((A_HEADS, A_V_DIM, KV_RANK), lambda i, j: (0, 0, 0)),
            pl.BlockSpec((A_HEADS * A_V_DIM, d), lambda i, j: (0, 0)),
            pl.BlockSpec((1, tm, d), lambda i, j: (i, j, 0)),
            pl.BlockSpec((1, 6, d), lambda i, j: (i, 0, 0)),
        ],
        out_specs=pl.BlockSpec((1, tm, d), lambda i, j: (i, j, 0)),
        out_shape=jax.ShapeDtypeStruct((b, t, d), F32),
        compiler_params=_cparams(("arbitrary", "arbitrary")),
        name="dsa_post",
    )(olatt, jnp.swapaxes(w_uv, 1, 2).astype(BF16), w_o.astype(BF16), x, mod)


CONV_HALO = 8


def _ffn_kernel(x_ref, mod_ref, g_ref, wu_ref, wv_ref, cw_ref, cb_ref, wout_ref, gfin_ref,
                o_ref, h_scr, acc_scr, u_scr, carry_scr, *, final_norm):
    ti = pl.program_id(1)
    f = pl.program_id(2)
    tm = x_ref.shape[1]
    mod = mod_ref[0]

    @pl.when(f == 0)
    def _():
        h_scr[...] = _norm_mod(x_ref[0], g_ref[...], mod[4:5], mod[3:4]).astype(BF16)
        acc_scr[...] = jnp.zeros(acc_scr.shape, F32)

    @pl.when(ti == 0)
    def _():
        carry_scr[f] = jnp.zeros(carry_scr.shape[1:], F32)

    h = h_scr[...]
    u = _dot(h, wu_ref[...])
    v = _dot(h, wv_ref[...])
    u_scr[0:CONV_HALO] = carry_scr[f]
    u_scr[CONV_HALO:CONV_HALO + tm] = u
    carry_scr[f] = u[tm - CONV_HALO:tm]
    cw = cw_ref[...]
    u1 = u_scr[CONV_HALO - 1:CONV_HALO - 1 + tm]
    u2 = u_scr[CONV_HALO - 2:CONV_HALO - 2 + tm]
    uc = cw[0:1] * u2 + cw[1:2] * u1 + cw[2:3] * u + cb_ref[...]
    act = 0.5 * uc * (1.0 + lax.erf(uc * (2.0 ** -0.5)))
    acc_scr[...] += _dot((act * v).astype(BF16), wout_ref[...])

    @pl.when(f == pl.num_programs(2) - 1)
    def _():
        y = x_ref[0] + mod[5:6] * acc_scr[...]
        if final_norm:
            y = _rms(y) * gfin_ref[...]
        o_ref[0] = y


def _conv_ffn(x, mod, g_ffn, w_in, conv_w, conv_b, w_out, g_final, final_norm):
    b, t, d = x.shape
    tm = min(ROW_TILE, t)
    fc = D_FF // FF_SPLIT
    assert fc % LANES == 0
    w_in_b = w_in.astype(BF16)
    kernel = functools.partial(_ffn_kernel, final_norm=final_norm)
    return pl.pallas_call(
        kernel,
        grid=(b, t // tm, FF_SPLIT),
        in_specs=[
            pl.BlockSpec((1, tm, d), lambda i, j, f: (i, j, 0)),
            pl.BlockSpec((1, 6, d), lambda i, j, f: (i, 0, 0)),
            pl.BlockSpec((1, d), lambda i, j, f: (0, 0)),
            pl.BlockSpec((d, fc), lambda i, j, f: (0, f)),
            pl.BlockSpec((d, fc), lambda i, j, f: (0, FF_SPLIT + f)),
            pl.BlockSpec((3, fc), lambda i, j, f: (0, f)),
            pl.BlockSpec((1, fc), lambda i, j, f: (0, f)),
            pl.BlockSpec((fc, d), lambda i, j, f: (f, 0)),
            pl.BlockSpec((1, d), lambda i, j, f: (0, 0)),
        ],
        out_specs=pl.BlockSpec((1, tm, d), lambda i, j, f: (i, j, 0)),
        out_shape=jax.ShapeDtypeStruct((b, t, d), F32),
        scratch_shapes=[
            pltpu.VMEM((tm, d), BF16),
            pltpu.VMEM((tm, d), F32),
            pltpu.VMEM((tm + CONV_HALO, fc), F32),
            pltpu.VMEM((FF_SPLIT, CONV_HALO, fc), F32),
        ],
        compiler_params=_cparams(("arbitrary", "arbitrary", "arbitrary")),
        name="conv_ffn_final" if final_norm else "conv_ffn",
    )(x, mod, g_ffn.reshape(1, d), w_in_b, w_in_b, conv_w, conv_b.reshape(1, D_FF),
      w_out.astype(BF16), g_final.reshape(1, d))


def _gla_pre_kernel(x_ref, mod_ref, g_ref, wqkr_ref, wv_ref, wvt_ref, wg_ref, wg2_ref, bg_ref,
                    q_ref, k_ref, r_ref, v_ref, vt_ref, la_ref):
    mod = mod_ref[0]
    h = _norm_mod(x_ref[0], g_ref[...], mod[1:2], mod[0:1])
    hb = h.astype(BF16)
    qkr = _dot(hb, wqkr_ref[...])
    q_ref[0] = (qkr[:, :G_KD] * (G_DK ** -0.5)).astype(BF16)
    k_ref[0] = qkr[:, G_KD:2 * G_KD].astype(BF16)
    r_ref[0] = qkr[:, 2 * G_KD:].astype(BF16)
    v_ref[0] = _dot(hb, wv_ref[...]).astype(BF16)
    vt_ref[0] = _dot_nt(wvt_ref[...], hb).astype(BF16)
    g_lr = _dot_hp(h, wg_ref[...])
    gate = _dot_hp(g_lr, wg2_ref[...]) + bg_ref[...]
    log_sig = jnp.minimum(gate, 0.0) - jnp.log1p(jnp.exp(-jnp.abs(gate)))
    la_ref[0] = log_sig / GATE_TAU


def _gla_pre(x, mod, g_mix, w_in, w_g2, b_g):
    b, t, d = x.shape
    tm = min(ROW_TILE, t)
    w_q, w_k, w_v, w_r, w_g = jnp.split(w_in, [G_KD, 2 * G_KD, 2 * G_KD + G_VD, 2 * G_KD + 2 * G_VD], axis=1)
    w_qkr = jnp.concatenate([w_q, w_k, w_r], axis=1).astype(BF16)
    w_g_pad = jnp.concatenate([w_g, jnp.zeros((d, LANES - G_RANK), F32)], axis=1)
    w_g2_pad = jnp.concatenate([w_g2, jnp.zeros((LANES - G_RANK, G_KD), F32)], axis=0)
    const = lambda *shape: pl.BlockSpec(shape, lambda i, j: (0,) * len(shape))
    row = lambda width: pl.BlockSpec((1, tm, width), lambda i, j: (i, j, 0))
    return pl.pallas_call(
        _gla_pre_kernel,
        grid=(b, t // tm),
        in_specs=[
            row(d),
            pl.BlockSpec((1, 6, d), lambda i, j: (i, 0, 0)),
            const(1, d),
            const(d, 2 * G_KD + G_VD),
            const(d, G_VD),
            const(G_VD, d),
            const(d, LANES),
            const(LANES, G_KD),
            const(1, G_KD),
        ],
        out_specs=[row(G_KD), row(G_KD), row(G_VD), row(G_VD),
                   pl.BlockSpec((1, G_VD, tm), lambda i, j: (i, 0, j)), row(G_KD)],
        out_shape=[
            jax.ShapeDtypeStruct((b, t, G_KD), BF16),
            jax.ShapeDtypeStruct((b, t, G_KD), BF16),
            jax.ShapeDtypeStruct((b, t, G_VD), BF16),
            jax.ShapeDtypeStruct((b, t, G_VD), BF16),
            jax.ShapeDtypeStruct((b, G_VD, t), BF16),
            jax.ShapeDtypeStruct((b, t, G_KD), F32),
        ],
        compiler_params=_cparams(("arbitrary", "arbitrary")),
        name="gla_pre",
    )(x, mod, g_mix.reshape(1, d), w_qkr, w_v.astype(BF16), w_v.T.astype(BF16), w_g_pad, w_g2_pad,
      b_g.reshape(1, G_KD))


GLA_LEVELS = tuple(1 << p for p in range(GLA_CHUNK.bit_length() - 1))


def _gla_kernel(q_ref, k_ref, g_ref, v_ref, vt_ref, o_ref, st_scr):
    n_ch = q_ref.shape[1] // GLA_CHUNK
    cl = GLA_CHUNK

    @pl.when(pl.program_id(2) == 0)
    def _():
        st_scr[...] = jnp.zeros(st_scr.shape, F32)

    def lanes(ref):
        x = ref[0]
        return jnp.concatenate([x[c * cl:(c + 1) * cl] for c in range(n_ch)], axis=1)

    q = lanes(q_ref).astype(F32)
    k = lanes(k_ref).astype(F32)
    g1, g2, g3 = _split3(lanes(g_ref))

    r = lax.broadcasted_iota(I32, ((len(GLA_LEVELS) + 1) * cl, cl), 0)
    cidx = lax.broadcasted_iota(I32, ((len(GLA_LEVELS) + 1) * cl, cl), 1)
    i_loc = r & (cl - 1)
    lvl = lax.shift_right_logical(r, I32(cl.bit_length() - 1))
    half = lax.shift_left(I32(1), lvl)
    ref_row = jnp.where(lvl < len(GLA_LEVELS), (i_loc & ~(2 * half - 1)) | (half - 1), -1)
    sel = (jnp.where(cidx <= i_loc, 1.0, 0.0) - jnp.where(cidx <= ref_row, 1.0, 0.0)).astype(BF16)
    d_all = _dot(sel, g1) + (_dot(sel, g2) + _dot(sel, g3))

    row = lax.broadcasted_iota(I32, (cl, 1), 0)
    xor = lax.broadcasted_iota(I32, (cl, cl), 0) ^ lax.broadcasted_iota(I32, (cl, cl), 1)

    a = [jnp.where(xor == 0, _dot_nt(q[:, c * G_DK:(c + 1) * G_DK].astype(BF16),
                                     k[:, c * G_DK:(c + 1) * G_DK].astype(BF16)), 0.0)
         for c in range(n_ch)]
    for li, s in enumerate(GLA_LEVELS):
        dl = d_all[li * cl:(li + 1) * cl]
        hi = (row & s) != 0
        e = jnp.exp(jnp.where(hi, dl, -dl))
        qt = jnp.where(hi, q * e, 0.0).astype(BF16)
        kt = jnp.where(hi, 0.0, k * e).astype(BF16)
        for c in range(n_ch):
            p = _dot_nt(qt[:, c * G_DK:(c + 1) * G_DK], kt[:, c * G_DK:(c + 1) * G_DK])
            a[c] = a[c] + jnp.where(xor < 2 * s, p, 0.0)

    b = d_all[len(GLA_LEVELS) * cl:]
    b_last = b[cl - 1:cl]
    q_hat = (q * jnp.exp(b)).astype(BF16)
    k_hat = (k * jnp.exp(b_last - b)).astype(BF16)
    decay = jnp.exp(b_last)

    st = st_scr[...]
    v_all = v_ref[0]
    vt_all = vt_ref[0]
    outs = []
    for c in range(n_ch):
        sl = slice(c * G_DK, (c + 1) * G_DK)
        o_inter = _dot_nt(q_hat[:, sl], st.astype(BF16))
        o_intra = _dot(a[c].astype(BF16), v_all[c * cl:(c + 1) * cl])
        outs.append(o_inter + o_intra)
        st = st * decay[:, sl] + _dot(vt_all[:, c * cl:(c + 1) * cl], k_hat[:, sl])
    st_scr[...] = st
    o_ref[0] = jnp.concatenate(outs, axis=0).astype(BF16)


def _gla_core(q, k, la, v, vt):
    b, t, _ = q.shape
    tm = min(GLA_TILE, t)
    assert G_DK == LANES and tm % GLA_CHUNK == 0
    return pl.pallas_call(
        _gla_kernel,
        grid=(b, G_HEADS, t // tm),
        in_specs=[
            pl.BlockSpec((1, tm, G_DK), lambda i, h, j: (i, j, h)),
            pl.BlockSpec((1, tm, G_DK), lambda i, h, j: (i, j, h)),
            pl.BlockSpec((1, tm, G_DK), lambda i, h, j: (i, j, h)),
            pl.BlockSpec((1, tm, G_DV), lambda i, h, j: (i, j, h)),
            pl.BlockSpec((1, G_DV, tm), lambda i, h, j: (i, h, j)),
        ],
        out_specs=pl.BlockSpec((1, tm, G_DV), lambda i, h, j: (i, j, h)),
        out_shape=jax.ShapeDtypeStruct((b, t, G_VD), BF16),
        scratch_shapes=[pltpu.VMEM((G_DV, G_DK), F32)],
        compiler_params=_cparams(("arbitrary", "arbitrary", "arbitrary")),
        name="gla_core",
    )(q, k, la, v, vt)


def _gla_post_kernel(o_ref, r_ref, gn_ref, wo_ref, x_ref, mod_ref, out_ref):
    o = o_ref[0].astype(F32)
    gn = gn_ref[...]
    heads = [_rms(o[:, hh * G_DV:(hh + 1) * G_DV]) * gn[:, hh * G_DV:(hh + 1) * G_DV]
             for hh in range(G_HEADS)]
    r = r_ref[0].astype(F32)
    z = jnp.concatenate(heads, axis=1) * (r * jax.nn.sigmoid(r))
    out_ref[0] = x_ref[0] + mod_ref[0][2:3] * _dot(z.astype(BF16), wo_ref[...])


def _gla_post(o, r, g_norm, w_o, x, mod):
    b, t, d = x.shape
    tm = min(ROW_TILE, t)
    row = lambda width: pl.BlockSpec((1, tm, width), lambda i, j: (i, j, 0))
    return pl.pallas_call(
        _gla_post_kernel,
        grid=(b, t // tm),
        in_specs=[row(G_VD), row(G_VD),
                  pl.BlockSpec((1, G_VD), lambda i, j: (0, 0)),
                  pl.BlockSpec((G_VD, d), lambda i, j: (0, 0)),
                  row(d),
                  pl.BlockSpec((1, 6, d), lambda i, j: (i, 0, 0))],
        out_specs=row(d),
        out_shape=jax.ShapeDtypeStruct((b, t, d), F32),
        compiler_params=_cparams(("arbitrary", "arbitrary")),
        name="gla_post",
    )(o, r, g_norm.reshape(1, G_VD), w_o.astype(BF16), x, mod)


def kernel(x, c, rel_bias, a_w_in, a_g_cq, a_g_ckv, a_w_uq, a_w_uk, a_w_uv, a_w_qi, a_w_o, b_w_in, b_w_g2, b_b_g, b_g_norm, b_w_o, ada_w, ada_b, g_mix, g_ffn, f_w_in, f_conv_w, f_conv_b, f_w_out, g_final):
    b, t, d = x.shape
    depth = ada_w.shape[0]
    assert d == D_MODEL and depth == 2 and t % ROW_TILE == 0
    mod = _modulation(c, ada_w, ada_b).reshape(depth, b, 6, d)

    qlatt, qi3t, ki3, widxt, ckv, ckvt = _dsa_pre(x, mod[0], g_mix[0], a_w_in[0], a_g_cq[0], a_g_ckv[0],
                                                  a_w_uq[0], a_w_uk[0], a_w_qi[0])
    olatt = _dsa_attn(qi3t, ki3, widxt, qlatt, ckv, ckvt, _bias_tiles(rel_bias))
    x = _dsa_post(olatt, a_w_uv[0], a_w_o[0], x, mod[0])
    x = _conv_ffn(x, mod[0], g_ffn[0], f_w_in[0], f_conv_w[0], f_conv_b[0], f_w_out[0], g_final, False)

    q, k, r, v, vt, la = _gla_pre(x, mod[1], g_mix[1], b_w_in[0], b_w_g2[0], b_b_g[0])
    o = _gla_core(q, k, la, v, vt)
    x = _gla_post(o, r, b_g_norm[0], b_w_o[0], x, mod[1])
    return _conv_ffn(x, mod[1], g_ffn[1], f_w_in[1], f_conv_w[1], f_conv_b[1], f_w_out[1], g_final, True)
```

```python
import functools
import math

import numpy as np
import jax
import jax.numpy as jnp
from jax import lax
from jax.experimental import pallas as pl
from jax.experimental.pallas import tpu as pltpu

F32 = jnp.float32
BF16 = jnp.bfloat16
I32 = jnp.int32

D_MODEL = 1024
A_HEADS = 8
A_HEAD_DIM = 128
A_V_DIM = 128
Q_RANK = 256
KV_RANK = 256
IDX_HEADS = 8
IDX_DIM = 64
TOPK_MAX = 256
NUM_BUCKETS = 32
MAX_DISTANCE = 128
G_HEADS = 4
G_KD = 512
G_VD = 1024
G_DK = G_KD // G_HEADS
G_DV = G_VD // G_HEADS
G_RANK = 16
GATE_TAU = 16.0
D_FF = 2816
EPS = 1e-6

LANES = 128
VMEM_LIMIT_BYTES = 56 * 1024 * 1024

ROW_TILE = 512
ATT_TILE = 256
GLA_TILE = 512
GLA_CHUNK = 128
FF_SPLIT = 2
NEG_INF = float("-inf")
LOG2E = math.log2(math.e)


def _cparams(sem):
    return pltpu.CompilerParams(dimension_semantics=sem, vmem_limit_bytes=VMEM_LIMIT_BYTES)


def _dot(a, b):
    return lax.dot_general(a, b, (((1,), (0,)), ((), ())), preferred_element_type=F32)


def _dot_nt(a, b):
    return lax.dot_general(a, b, (((1,), (1,)), ((), ())), preferred_element_type=F32)


def _split2(a):
    hi = a.astype(BF16)
    lo = (a - hi.astype(F32)).astype(BF16)
    return hi, lo


def _split3(a):
    p1 = a.astype(BF16)
    r1 = a - p1.astype(F32)
    p2 = r1.astype(BF16)
    p3 = (r1 - p2.astype(F32)).astype(BF16)
    return p1, p2, p3


def _dot_hp(a, b):
    ah, al = _split2(a)
    bh, bl = _split2(b)
    return _dot(ah, bh) + (_dot(ah, bl) + _dot(al, bh))


def _dot_hp_nt(a, b):
    ah, al = _split2(a)
    bh, bl = _split2(b)
    return _dot_nt(ah, bh) + (_dot_nt(ah, bl) + _dot_nt(al, bh))


def _hi_lo_concat(a, order):
    hi = a.astype(BF16).astype(F32)
    parts = (hi, a - hi)
    return jnp.concatenate([parts[o] for o in order], axis=1).astype(BF16)


def _rms(x):
    return x * lax.rsqrt(jnp.mean(x * x, axis=-1, keepdims=True) + EPS)


def _norm_mod(x, g, sc, sh):
    return (_rms(x) * g) * (1.0 + sc) + sh


def _mod_kernel(c_ref, w_ref, b_ref, o_ref):
    c = c_ref[...]
    cond = c * jax.nn.sigmoid(c)
    o_ref[0] = _dot_hp(cond, w_ref[0]) + b_ref[0]


def _modulation(c, ada_w, ada_b):
    depth, d, six_d = ada_w.shape
    b = c.shape[0]
    n_col = six_d // d
    return pl.pallas_call(
        _mod_kernel,
        grid=(depth, n_col),
        in_specs=[
            pl.BlockSpec((b, d), lambda l, j: (0, 0)),
            pl.BlockSpec((1, d, d), lambda l, j: (l, 0, j)),
            pl.BlockSpec((1, 1, d), lambda l, j: (l, 0, j)),
        ],
        out_specs=pl.BlockSpec((1, b, d), lambda l, j: (l, 0, j)),
        out_shape=jax.ShapeDtypeStruct((depth, b, six_d), F32),
        compiler_params=_cparams(("arbitrary", "arbitrary")),
        name="adaln_mod",
    )(c, ada_w, ada_b.reshape(depth, 1, six_d))


def _rms_cols(xt):
    return xt * lax.rsqrt(jnp.mean(xt * xt, axis=0, keepdims=True) + EPS)


def _dsa_pre_kernel(x_ref, mod_ref, g_ref, wk_ref, wkv_ref, wkvt_ref, wqt_ref, gcq_ref, gckv_ref,
                    gckvc_ref, wuqt_ref, wukt_ref, wqit_ref,
                    qlatt_ref, qi3t_ref, ki3_ref, widxt_ref, ckv_ref, ckvt_ref):
    mod = mod_ref[0]
    h = _norm_mod(x_ref[0], g_ref[...], mod[1:2], mod[0:1])
    hb = h.astype(BF16)

    k_idx = _dot_hp(h, wk_ref[...])[:, :IDX_DIM]
    ki3_ref[0] = _hi_lo_concat(k_idx, (0, 1, 0))
    ckv_ref[0] = (_rms(_dot(hb, wkv_ref[...])) * gckv_ref[...]).astype(BF16)
    c_kvt = (_rms_cols(_dot_nt(wkvt_ref[...], hb)) * gckvc_ref[...]).astype(BF16)
    for j in range(ckvt_ref.shape[1]):
        ckvt_ref[0, j] = c_kvt[:, j * ATT_TILE:(j + 1) * ATT_TILE]

    hpt = _dot_hp_nt(wqt_ref[...], h)
    widxt_ref[0] = hpt[Q_RANK:] * (IDX_HEADS ** -0.5 * IDX_DIM ** -0.5)
    c_qt = _rms_cols(hpt[:Q_RANK]) * gcq_ref[...]
    qt = _dot(wuqt_ref[...], c_qt.astype(BF16))
    for hh in range(A_HEADS):
        qh = qt[hh * A_HEAD_DIM:(hh + 1) * A_HEAD_DIM].astype(BF16)
        qlatt_ref[0, hh] = (_dot(wukt_ref[hh], qh) * (A_HEAD_DIM ** -0.5 * LOG2E)).astype(BF16)
    q_idxt = _dot_hp(wqit_ref[...], c_qt)
    for hh in range(IDX_HEADS):
        piece = q_idxt[hh * IDX_DIM:(hh + 1) * IDX_DIM]
        hi = piece.astype(BF16).astype(F32)
        qi3t_ref[0, hh] = jnp.concatenate([hi, hi, piece - hi], axis=0).astype(BF16)


def _dsa_pre(x, mod, g_mix, w_in, g_cq, g_ckv, w_uq, w_uk, w_qi):
    b, t, d = x.shape
    tm = min(ROW_TILE, t)
    w_cq, w_ckv, w_ki, w_wi = jnp.split(w_in, [Q_RANK, Q_RANK + KV_RANK, Q_RANK + KV_RANK + IDX_DIM], axis=1)
    w_k = jnp.concatenate([w_ki, jnp.zeros((d, LANES - IDX_DIM), F32)], axis=1)
    w_qt = jnp.concatenate([w_cq, w_wi], axis=1).T
    const = lambda *shape: pl.BlockSpec(shape, lambda i, j: (0,) * len(shape))
    return pl.pallas_call(
        _dsa_pre_kernel,
        grid=(b, t // tm),
        in_specs=[
            pl.BlockSpec((1, tm, d), lambda i, j: (i, j, 0)),
            pl.BlockSpec((1, 6, d), lambda i, j: (i, 0, 0)),
            const(1, d),
            const(d, LANES),
            const(d, KV_RANK),
            const(KV_RANK, d),
            const(Q_RANK + IDX_HEADS, d),
            const(Q_RANK, 1),
            const(1, KV_RANK),
            const(KV_RANK, 1),
            const(A_HEADS * A_HEAD_DIM, Q_RANK),
            const(A_HEADS, KV_RANK, A_HEAD_DIM),
            const(IDX_HEADS * IDX_DIM, Q_RANK),
        ],
        out_specs=[
            pl.BlockSpec((1, A_HEADS, KV_RANK, tm), lambda i, j: (i, 0, 0, j)),
            pl.BlockSpec((1, IDX_HEADS, 3 * IDX_DIM, tm), lambda i, j: (i, 0, 0, j)),
            pl.BlockSpec((1, tm, 3 * IDX_DIM), lambda i, j: (i, j, 0)),
            pl.BlockSpec((1, IDX_HEADS, tm), lambda i, j: (i, 0, j)),
            pl.BlockSpec((1, tm, KV_RANK), lambda i, j: (i, j, 0)),
            pl.BlockSpec((1, tm // ATT_TILE, KV_RANK, ATT_TILE), lambda i, j: (i, j, 0, 0)),
        ],
        out_shape=[
            jax.ShapeDtypeStruct((b, A_HEADS, KV_RANK, t), BF16),
            jax.ShapeDtypeStruct((b, IDX_HEADS, 3 * IDX_DIM, t), BF16),
            jax.ShapeDtypeStruct((b, t, 3 * IDX_DIM), BF16),
            jax.ShapeDtypeStruct((b, IDX_HEADS, t), F32),
            jax.ShapeDtypeStruct((b, t, KV_RANK), BF16),
            jax.ShapeDtypeStruct((b, t // ATT_TILE, KV_RANK, ATT_TILE), BF16),
        ],
        compiler_params=_cparams(("arbitrary", "arbitrary")),
        name="dsa_pre",
    )(x, mod, g_mix.reshape(1, d), w_k, w_ckv.astype(BF16), w_ckv.T.astype(BF16), w_qt,
      g_cq.reshape(Q_RANK, 1), g_ckv.reshape(1, KV_RANK), g_ckv.reshape(KV_RANK, 1),
      w_uq.T.astype(BF16), jnp.swapaxes(w_uk, 1, 2).astype(BF16), w_qi.T)


def _bucket_starts():
    exact = NUM_BUCKETS // 2
    starts = list(range(1, exact + 1))
    d = exact
    for m in range(exact + 1, NUM_BUCKETS):
        while exact + int(math.log(d / exact) / math.log(MAX_DISTANCE / exact) * (NUM_BUCKETS - exact)) < m:
            d += 1
        starts.append(d)
    return tuple(starts)


BUCKET_STARTS = _bucket_starts()
FAR_DISTANCE = BUCKET_STARTS[-1]


def _bias_kernel(rb_ref, o_ref):
    hh = pl.program_id(0)
    key = lax.broadcasted_iota(I32, (ATT_TILE, ATT_TILE), 0)
    qry = lax.broadcasted_iota(I32, (ATT_TILE, ATT_TILE), 1)
    far = rb_ref[NUM_BUCKETS - 1, hh]
    for kind, offset in ((0, ATT_TILE), (1, 0)):
        dist = jnp.maximum(qry - key + offset, 0)
        acc = jnp.full((ATT_TILE, ATT_TILE), 0.0, F32) + far
        for m in range(NUM_BUCKETS - 2, -1, -1):
            acc = jnp.where(dist < BUCKET_STARTS[m], rb_ref[m, hh], acc)
        o_ref[kind, 0] = (acc - far) * LOG2E


def _bias_tiles(rel_bias):
    assert ATT_TILE >= FAR_DISTANCE
    return pl.pallas_call(
        _bias_kernel,
        grid=(A_HEADS,),
        in_specs=[pl.BlockSpec(memory_space=pltpu.SMEM)],
        out_specs=pl.BlockSpec((2, 1, ATT_TILE, ATT_TILE), lambda h: (0, h, 0, 0)),
        out_shape=jax.ShapeDtypeStruct((2, A_HEADS, ATT_TILE, ATT_TILE), F32),
        compiler_params=_cparams(("arbitrary",)),
        name="t5_bias_tiles",
    )(rel_bias)


def _dsa_attn_kernel(qi3t_ref, ki3_ref, widxt_ref, qlatt_ref, ckv_ref, ckvt_ref, bias_ref, o_ref,
                     s_scr, tie_scr, m_scr, l_scr, acc_scr, *, n_sel, idx_bits):
    qi = pl.program_id(1)
    n_chunks = qi + 1
    tq = ATT_TILE
    q_pos = qi * tq + lax.broadcasted_iota(I32, (1, tq), 1)
    k_loc = lax.broadcasted_iota(I32, (tq, 1), 0)

    w_all = widxt_ref[0]

    def score_chunk(c, carry):
        k3 = ki3_ref[0, pl.ds(pl.multiple_of(c * tq, tq), tq), :]
        sc = jnp.zeros((tq, tq), F32)
        for hh in range(IDX_HEADS):
            sc = sc + w_all[hh:hh + 1, :] * jnp.maximum(_dot(k3, qi3t_ref[0, hh]), 0.0)
        s_scr[c] = jnp.where(c * tq + k_loc <= q_pos, sc, NEG_INF)
        return carry

    lax.fori_loop(0, n_chunks, score_chunk, 0)

    @pl.when((n_chunks & 1) == 1)
    def _():
        s_scr[n_chunks] = jnp.full((tq, tq), NEG_INF, F32)

    def count(pred):
        def body(j, acc):
            c = 2 * j
            ind = (jnp.where(pred(s_scr[c], c * tq + k_loc), 1.0, 0.0).astype(F32)
                   + jnp.where(pred(s_scr[c + 1], (c + 1) * tq + k_loc), 1.0, 0.0).astype(F32))
            return acc + jnp.sum(ind.reshape(tq // 8, 8, tq), axis=0)
        acc = lax.fori_loop(0, lax.shift_right_logical(n_chunks + 1, I32(1)), body, jnp.zeros((8, tq), F32))
        return jnp.sum(acc, axis=0, keepdims=True)

    def key_to_float(key):
        bits = jnp.where(key < 0, key ^ I32(-2 ** 31), ~key)
        return lax.bitcast_convert_type(bits, F32)

    def tau_bit(i, key):
        cand = key | lax.shift_left(I32(1), I32(31) - i)
        cand_f = key_to_float(cand)
        cnt = count(lambda s, kp: s >= cand_f)
        return jnp.where(cnt >= n_sel, cand, key)

    key = lax.fori_loop(0, 32, tau_bit, jnp.zeros((1, tq), I32))
    fewer = (q_pos + 1) < n_sel
    tau = jnp.where(fewer, NEG_INF, key_to_float(key))

    n_gt = count(lambda s, kp: s > tau)
    need = jnp.where(fewer, 4.0 * ki3_ref.shape[1], n_sel - n_gt)

    def tie_bit(i, x):
        cand = x | lax.shift_left(I32(1), I32(idx_bits - 1) - i)
        cnt = count(lambda s, kp: (s == tau) & (kp < cand))
        return jnp.where(cnt < need, cand, x)

    tie_scr[...] = jnp.full(tie_scr.shape, 2 ** idx_bits - 1, I32)
    n_eq = count(lambda s, kp: s == tau)

    @pl.when(jnp.max(n_eq - need) > 0.0)
    def _():
        tie_scr[...] = lax.fori_loop(0, idx_bits, tie_bit, jnp.zeros((1, tq), I32))

    last_tie = tie_scr[...]

    def mask_chunk(c, carry):
        s = s_scr[c]
        k_pos = c * tq + k_loc
        sel = (k_pos <= q_pos) & ((s > tau) | ((s == tau) & (k_pos <= last_tie)))
        s_scr[c] = jnp.where(sel, 0.0, NEG_INF)
        return carry

    lax.fori_loop(0, n_chunks, mask_chunk, 0)

    m_scr[...] = jnp.full(m_scr.shape, NEG_INF, F32)
    l_scr[...] = jnp.zeros(l_scr.shape, F32)
    acc_scr[...] = jnp.zeros(acc_scr.shape, F32)

    def attend(c, bias_kind):
        kv = ckv_ref[0, pl.ds(pl.multiple_of(c * tq, tq), tq), :]
        kvt = ckvt_ref[0, c]
        mask = s_scr[c]
        for hh in range(A_HEADS):
            logit = _dot(kv, qlatt_ref[0, hh]) + mask
            if bias_kind is not None:
                logit = logit + bias_ref[bias_kind, hh]
            m_old = m_scr[hh:hh + 1]
            m_new = jnp.maximum(m_old, jnp.max(logit, axis=0, keepdims=True))
            m_safe = jnp.where(m_new == NEG_INF, 0.0, m_new)
            alpha = jnp.exp2(m_old - m_safe)
            p = jnp.exp2(logit - m_safe)
            l_scr[hh:hh + 1] = alpha * l_scr[hh:hh + 1] + jnp.sum(p, axis=0, keepdims=True)
            acc_scr[hh] = alpha * acc_scr[hh] + _dot(kvt, p.astype(BF16))
            m_scr[hh:hh + 1] = m_new

    def far_chunk(c, carry):
        attend(c, None)
        return carry

    lax.fori_loop(0, qi - 1, far_chunk, 0)

    @pl.when(qi >= 1)
    def _():
        attend(qi - 1, 0)

    attend(qi, 1)
    for hh in range(A_HEADS):
        o_ref[0, hh] = (acc_scr[hh] / l_scr[hh:hh + 1]).astype(BF16)


def _dsa_attn(qi3t, ki3, widxt, qlatt, ckv, ckvt, bias):
    b, t, _ = ckv.shape
    tq = ATT_TILE
    n_sel = min(TOPK_MAX, t // 4)
    assert (t // tq) % 2 == 0
    idx_bits = max(1, (t - 1).bit_length())
    kernel = functools.partial(_dsa_attn_kernel, n_sel=n_sel, idx_bits=idx_bits)
    return pl.pallas_call(
        kernel,
        grid=(b, t // tq),
        in_specs=[
            pl.BlockSpec((1, IDX_HEADS, 3 * IDX_DIM, tq), lambda i, j: (i, 0, 0, j)),
            pl.BlockSpec((1, t, 3 * IDX_DIM), lambda i, j: (i, 0, 0)),
            pl.BlockSpec((1, IDX_HEADS, tq), lambda i, j: (i, 0, j)),
            pl.BlockSpec((1, A_HEADS, KV_RANK, tq), lambda i, j: (i, 0, 0, j)),
            pl.BlockSpec((1, t, KV_RANK), lambda i, j: (i, 0, 0)),
            pl.BlockSpec((1, t // tq, KV_RANK, tq), lambda i, j: (i, 0, 0, 0)),
            pl.BlockSpec((2, A_HEADS, tq, tq), lambda i, j: (0, 0, 0, 0)),
        ],
        out_specs=pl.BlockSpec((1, A_HEADS, KV_RANK, tq), lambda i, j: (i, 0, 0, j)),
        out_shape=jax.ShapeDtypeStruct((b, A_HEADS, KV_RANK, t), BF16),
        scratch_shapes=[
            pltpu.VMEM((t // tq, tq, tq), F32),
            pltpu.VMEM((1, tq), I32),
            pltpu.VMEM((A_HEADS, tq), F32),
            pltpu.VMEM((A_HEADS, tq), F32),
            pltpu.VMEM((A_HEADS, KV_RANK, tq), F32),
        ],
        compiler_params=_cparams(("arbitrary", "arbitrary")),
        name="dsa_attn",
    )(qi3t, ki3, widxt, qlatt, ckv, ckvt, bias)


def _dsa_post_kernel(olatt_ref, wuvt_ref, wo_ref, x_ref, mod_ref, o_ref):
    heads = [_dot(wuvt_ref[hh], olatt_ref[0, hh]).astype(BF16) for hh in range(A_HEADS)]
    ot = jnp.concatenate(heads, axis=0)
    y = lax.dot_general(ot, wo_ref[...], (((0,), (0,)), ((), ())), preferred_element_type=F32)
    o_ref[0] = x_ref[0] + mod_ref[0][2:3] * y


def _dsa_post(olatt, w_uv, w_o, x, mod):
    b, t, d = x.shape
    tm = min(ROW_TILE, t)
    return pl.pallas_call(
        _dsa_post_kernel,
        grid=(b, t // tm),
        in_specs=[
            pl.BlockSpec((1, A_HEADS, KV_RANK, tm), lambda i, j: (i, 0, 0, j)),
            pl.BlockSpec((A_HEADS, A_V_DIM, KV_RANK), lambda i, j: (0, 0, 0)),
            pl.BlockSpec((A_HEADS * A_V_DIM, d), lambda i, j: (0, 0)),
            pl.BlockSpec((1, tm, d), lambda i, j: (i, j, 0)),
            pl.BlockSpec((1, 6, d), lambda i, j: (i, 0, 0)),
        ],
        out_specs=pl.BlockSpec((1, tm, d), lambda i, j: (i, j, 0)),
        out_shape=jax.ShapeDtypeStruct((b, t, d), F32),
        compiler_params=_cparams(("arbitrary", "arbitrary")),
        name="dsa_post",
    )(olatt, jnp.swapaxes(w_uv, 1, 2).astype(BF16), w_o.astype(BF16), x, mod)


CONV_HALO = 8


def _ffn_kernel(x_ref, mod_ref, g_ref, wu_ref, wv_ref, cw_ref, cb_ref, wout_ref, gfin_ref,
                o_ref, h_scr, acc_scr, u_scr, carry_scr, *, final_norm):
    ti = pl.program_id(1)
    f = pl.program_id(2)
    tm = x_ref.shape[1]
    mod = mod_ref[0]

    @pl.when(f == 0)
    def _():
        h_scr[...] = _norm_mod(x_ref[0], g_ref[...], mod[4:5], mod[3:4]).astype(BF16)
        acc_scr[...] = jnp.zeros(acc_scr.shape, F32)

    @pl.when(ti == 0)
    def _():
        carry_scr[f] = jnp.zeros(carry_scr.shape[1:], F32)

    h = h_scr[...]
    u = _dot(h, wu_ref[...])
    v = _dot(h, wv_ref[...])
    u_scr[0:CONV_HALO] = carry_scr[f]
    u_scr[CONV_HALO:CONV_HALO + tm] = u
    carry_scr[f] = u[tm - CONV_HALO:tm]
    cw = cw_ref[...]
    u1 = u_scr[CONV_HALO - 1:CONV_HALO - 1 + tm]
    u2 = u_scr[CONV_HALO - 2:CONV_HALO - 2 + tm]
    uc = cw[0:1] * u2 + cw[1:2] * u1 + cw[2:3] * u + cb_ref[...]
    act = 0.5 * uc * (1.0 + lax.erf(uc * (2.0 ** -0.5)))
    acc_scr[...] += _dot((act * v).astype(BF16), wout_ref[...])

    @pl.when(f == pl.num_programs(2) - 1)
    def _():
        y = x_ref[0] + mod[5:6] * acc_scr[...]
        if final_norm:
            y = _rms(y) * gfin_ref[...]
        o_ref[0] = y


def _conv_ffn(x, mod, g_ffn, w_in, conv_w, conv_b, w_out, g_final, final_norm):
    b, t, d = x.shape
    tm = min(ROW_TILE, t)
    fc = D_FF // FF_SPLIT
    assert fc % LANES == 0
    w_in_b = w_in.astype(BF16)
    kernel = functools.partial(_ffn_kernel, final_norm=final_norm)
    return pl.pallas_call(
        kernel,
        grid=(b, t // tm, FF_SPLIT),
        in_specs=[
            pl.BlockSpec((1, tm, d), lambda i, j, f: (i, j, 0)),
            pl.BlockSpec((1, 6, d), lambda i, j, f: (i, 0, 0)),
            pl.BlockSpec((1, d), lambda i, j, f: (0, 0)),
            pl.BlockSpec((d, fc), lambda i, j, f: (0, f)),
            pl.BlockSpec((d, fc), lambda i, j, f: (0, FF_SPLIT + f)),
            pl.BlockSpec((3, fc), lambda i, j, f: (0, f)),
            pl.BlockSpec((1, fc), lambda i, j, f: (0, f)),
            pl.BlockSpec((fc, d), lambda i, j, f: (f, 0)),
            pl.BlockSpec((1, d), lambda i, j, f: (0, 0)),
        ],
        out_specs=pl.BlockSpec((1, tm, d), lambda i, j, f: (i, j, 0)),
        out_shape=jax.ShapeDtypeStruct((b, t, d), F32),
        scratch_shapes=[
            pltpu.VMEM((tm, d), BF16),
            pltpu.VMEM((tm, d), F32),
            pltpu.VMEM((tm + CONV_HALO, fc), F32),
            pltpu.VMEM((FF_SPLIT, CONV_HALO, fc), F32),
        ],
        compiler_params=_cparams(("arbitrary", "arbitrary", "arbitrary")),
        name="conv_ffn_final" if final_norm else "conv_ffn",
    )(x, mod, g_ffn.reshape(1, d), w_in_b, w_in_b, conv_w, conv_b.reshape(1, D_FF),
      w_out.astype(BF16), g_final.reshape(1, d))


def _gla_pre_kernel(x_ref, mod_ref, g_ref, wqkr_ref, wv_ref, wvt_ref, wg_ref, wg2_ref, bg_ref,
                    q_ref, k_ref, r_ref, v_ref, vt_ref, la_ref):
    mod = mod_ref[0]
    h = _norm_mod(x_ref[0], g_ref[...], mod[1:2], mod[0:1])
    hb = h.astype(BF16)
    qkr = _dot(hb, wqkr_ref[...])
    q_ref[0] = (qkr[:, :G_KD] * (G_DK ** -0.5)).astype(BF16)
    k_ref[0] = qkr[:, G_KD:2 * G_KD].astype(BF16)
    r_ref[0] = qkr[:, 2 * G_KD:].astype(BF16)
    v_ref[0] = _dot(hb, wv_ref[...]).astype(BF16)
    vt_ref[0] = _dot_nt(wvt_ref[...], hb).astype(BF16)
    g_lr = _dot_hp(h, wg_ref[...])
    gate = _dot_hp(g_lr, wg2_ref[...]) + bg_ref[...]
    log_sig = jnp.minimum(gate, 0.0) - jnp.log1p(jnp.exp(-jnp.abs(gate)))
    la_ref[0] = log_sig / GATE_TAU


def _gla_pre(x, mod, g_mix, w_in, w_g2, b_g):
    b, t, d = x.shape
    tm = min(ROW_TILE, t)
    w_q, w_k, w_v, w_r, w_g = jnp.split(w_in, [G_KD, 2 * G_KD, 2 * G_KD + G_VD, 2 * G_KD + 2 * G_VD], axis=1)
    w_qkr = jnp.concatenate([w_q, w_k, w_r], axis=1).astype(BF16)
    w_g_pad = jnp.concatenate([w_g, jnp.zeros((d, LANES - G_RANK), F32)], axis=1)
    w_g2_pad = jnp.concatenate([w_g2, jnp.zeros((LANES - G_RANK, G_KD), F32)], axis=0)
    const = lambda *shape: pl.BlockSpec(shape, lambda i, j: (0,) * len(shape))
    row = lambda width: pl.BlockSpec((1, tm, width), lambda i, j: (i, j, 0))
    return pl.pallas_call(
        _gla_pre_kernel,
        grid=(b, t // tm),
        in_specs=[
            row(d),
            pl.BlockSpec((1, 6, d), lambda i, j: (i, 0, 0)),
            const(1, d),
            const(d, 2 * G_KD + G_VD),
            const(d, G_VD),
            const(G_VD, d),
            const(d, LANES),
            const(LANES, G_KD),
            const(1, G_KD),
        ],
        out_specs=[row(G_KD), row(G_KD), row(G_VD), row(G_VD),
                   pl.BlockSpec((1, G_VD, tm), lambda i, j: (i, 0, j)), row(G_KD)],
        out_shape=[
            jax.ShapeDtypeStruct((b, t, G_KD), BF16),
            jax.ShapeDtypeStruct((b, t, G_KD), BF16),
            jax.ShapeDtypeStruct((b, t, G_VD), BF16),
            jax.ShapeDtypeStruct((b, t, G_VD), BF16),
            jax.ShapeDtypeStruct((b, G_VD, t), BF16),
            jax.ShapeDtypeStruct((b, t, G_KD), F32),
        ],
        compiler_params=_cparams(("arbitrary", "arbitrary")),
        name="gla_pre",
    )(x, mod, g_mix.reshape(1, d), w_qkr, w_v.astype(BF16), w_v.T.astype(BF16), w_g_pad, w_g2_pad,
      b_g.reshape(1, G_KD))


GLA_LEVELS = tuple(1 << p for p in range(GLA_CHUNK.bit_length() - 1))


def _gla_kernel(q_ref, k_ref, g_ref, v_ref, vt_ref, o_ref, st_scr):
    n_ch = q_ref.shape[1] // GLA_CHUNK
    cl = GLA_CHUNK

    @pl.when(pl.program_id(2) == 0)
    def _():
        st_scr[...] = jnp.zeros(st_scr.shape, F32)

    def lanes(ref):
        x = ref[0]
        return jnp.concatenate([x[c * cl:(c + 1) * cl] for c in range(n_ch)], axis=1)

    q = lanes(q_ref).astype(F32)
    k = lanes(k_ref).astype(F32)
    g1, g2, g3 = _split3(lanes(g_ref))

    r = lax.broadcasted_iota(I32, ((len(GLA_LEVELS) + 1) * cl, cl), 0)
    cidx = lax.broadcasted_iota(I32, ((len(GLA_LEVELS) + 1) * cl, cl), 1)
    i_loc = r & (cl - 1)
    lvl = lax.shift_right_logical(r, I32(cl.bit_length() - 1))
    half = lax.shift_left(I32(1), lvl)
    ref_row = jnp.where(lvl < len(GLA_LEVELS), (i_loc & ~(2 * half - 1)) | (half - 1), -1)
    sel = (jnp.where(cidx <= i_loc, 1.0, 0.0) - jnp.where(cidx <= ref_row, 1.0, 0.0)).astype(BF16)
    d_all = _dot(sel, g1) + (_dot(sel, g2) + _dot(sel, g3))

    row = lax.broadcasted_iota(I32, (cl, 1), 0)
    xor = lax.broadcasted_iota(I32, (cl, cl), 0) ^ lax.broadcasted_iota(I32, (cl, cl), 1)

    a = [jnp.where(xor == 0, _dot_nt(q[:, c * G_DK:(c + 1) * G_DK].astype(BF16),
                                     k[:, c * G_DK:(c + 1) * G_DK].astype(BF16)), 0.0)
         for c in range(n_ch)]
    for li, s in enumerate(GLA_LEVELS):
        dl = d_all[li * cl:(li + 1) * cl]
        hi = (row & s) != 0
        e = jnp.exp(jnp.where(hi, dl, -dl))
        qt = jnp.where(hi, q * e, 0.0).astype(BF16)
        kt = jnp.where(hi, 0.0, k * e).astype(BF16)
        for c in range(n_ch):
            p = _dot_nt(qt[:, c * G_DK:(c + 1) * G_DK], kt[:, c * G_DK:(c + 1) * G_DK])
            a[c] = a[c] + jnp.where(xor < 2 * s, p, 0.0)

    b = d_all[len(GLA_LEVELS) * cl:]
    b_last = b[cl - 1:cl]
    q_hat = (q * jnp.exp(b)).astype(BF16)
    k_hat = (k * jnp.exp(b_last - b)).astype(BF16)
    decay = jnp.exp(b_last)

    st = st_scr[...]
    v_all = v_ref[0]
    vt_all = vt_ref[0]
    outs = []
    for c in range(n_ch):
        sl = slice(c * G_DK, (c + 1) * G_DK)
        o_inter = _dot_nt(q_hat[:, sl], st.astype(BF16))
        o_intra = _dot(a[c].astype(BF16), v_all[c * cl:(c + 1) * cl])
        outs.append(o_inter + o_intra)
        st = st * decay[:, sl] + _dot(vt_all[:, c * cl:(c + 1) * cl], k_hat[:, sl])
    st_scr[...] = st
    o_ref[0] = jnp.concatenate(outs, axis=0).astype(BF16)


def _gla_core(q, k, la, v, vt):
    b, t, _ = q.shape
    tm = min(GLA_TILE, t)
    assert G_DK == LANES and tm % GLA_CHUNK == 0
    return pl.pallas_call(
        _gla_kernel,
        grid=(b, G_HEADS, t // tm),
        in_specs=[
            pl.BlockSpec((1, tm, G_DK), lambda i, h, j: (i, j, h)),
            pl.BlockSpec((1, tm, G_DK), lambda i, h, j: (i, j, h)),
            pl.BlockSpec((1, tm, G_DK), lambda i, h, j: (i, j, h)),
            pl.BlockSpec((1, tm, G_DV), lambda i, h, j: (i, j, h)),
            pl.BlockSpec((1, G_DV, tm), lambda i, h, j: (i, h, j)),
        ],
        out_specs=pl.BlockSpec((1, tm, G_DV), lambda i, h, j: (i, j, h)),
        out_shape=jax.ShapeDtypeStruct((b, t, G_VD), BF16),
        scratch_shapes=[pltpu.VMEM((G_DV, G_DK), F32)],
        compiler_params=_cparams(("arbitrary", "arbitrary", "arbitrary")),
        name="gla_core",
    )(q, k, la, v, vt)


def _gla_post_kernel(o_ref, r_ref, gn_ref, wo_ref, x_ref, mod_ref, out_ref):
    o = o_ref[0].astype(F32)
    gn = gn_ref[...]
    heads = [_rms(o[:, hh * G_DV:(hh + 1) * G_DV]) * gn[:, hh * G_DV:(hh + 1) * G_DV]
             for hh in range(G_HEADS)]
    r = r_ref[0].astype(F32)
    z = jnp.concatenate(heads, axis=1) * (r * jax.nn.sigmoid(r))
    out_ref[0] = x_ref[0] + mod_ref[0][2:3] * _dot(z.astype(BF16), wo_ref[...])


def _gla_post(o, r, g_norm, w_o, x, mod):
    b, t, d = x.shape
    tm = min(ROW_TILE, t)
    row = lambda width: pl.BlockSpec((1, tm, width), lambda i, j: (i, j, 0))
    return pl.pallas_call(
        _gla_post_kernel,
        grid=(b, t // tm),
        in_specs=[row(G_VD), row(G_VD),
                  pl.BlockSpec((1, G_VD), lambda i, j: (0, 0)),
                  pl.BlockSpec((G_VD, d), lambda i, j: (0, 0)),
                  row(d),
                  pl.BlockSpec((1, 6, d), lambda i, j: (i, 0, 0))],
        out_specs=row(d),
        out_shape=jax.ShapeDtypeStruct((b, t, d), F32),
        compiler_params=_cparams(("arbitrary", "arbitrary")),
        name="gla_post",
    )(o, r, g_norm.reshape(1, G_VD), w_o.astype(BF16), x, mod)


def kernel(x, c, rel_bias, a_w_in, a_g_cq, a_g_ckv, a_w_uq, a_w_uk, a_w_uv, a_w_qi, a_w_o, b_w_in, b_w_g2, b_b_g, b_g_norm, b_w_o, ada_w, ada_b, g_mix, g_ffn, f_w_in, f_conv_w, f_conv_b, f_w_out, g_final):
    b, t, d = x.shape
    depth = ada_w.shape[0]
    assert d == D_MODEL and depth == 2 and t % ROW_TILE == 0
    mod = _modulation(c, ada_w, ada_b).reshape(depth, b, 6, d)

    qlatt, qi3t, ki3, widxt, ckv, ckvt = _dsa_pre(x, mod[0], g_mix[0], a_w_in[0], a_g_cq[0], a_g_ckv[0],
                                                  a_w_uq[0], a_w_uk[0], a_w_qi[0])
    olatt = _dsa_attn(qi3t, ki3, widxt, qlatt, ckv, ckvt, _bias_tiles(rel_bias))
    x = _dsa_post(olatt, a_w_uv[0], a_w_o[0], x, mod[0])
    x = _conv_ffn(x, mod[0], g_ffn[0], f_w_in[0], f_conv_w[0], f_conv_b[0], f_w_out[0], g_final, False)

    q, k, r, v, vt, la = _gla_pre(x, mod[1], g_mix[1], b_w_in[0], b_w_g2[0], b_b_g[0])
    o = _gla_core(q, k, la, v, vt)
    x = _gla_post(o, r, b_g_norm[0], b_w_o[0], x, mod[1])
    return _conv_ffn(x, mod[1], g_ffn[1], f_w_in[1], f_conv_w[1], f_conv_b[1], f_w_out[1], g_final, True)
```

```python
import functools
import math

import numpy as np
import jax
import jax.numpy as jnp
from jax import lax
from jax.experimental import pallas as pl
from jax.experimental.pallas import tpu as pltpu

F32 = jnp.float32
BF16 = jnp.bfloat16
I32 = jnp.int32

D_MODEL = 1024
A_HEADS = 8
A_HEAD_DIM = 128
A_V_DIM = 128
Q_RANK = 256
KV_RANK = 256
IDX_HEADS = 8
IDX_DIM = 64
TOPK_MAX = 256
NUM_BUCKETS = 32
MAX_DISTANCE = 128
G_HEADS = 4
G_KD = 512
G_VD = 1024
G_DK = G_KD // G_HEADS
G_DV = G_VD // G_HEADS
G_RANK = 16
GATE_TAU = 16.0
D_FF = 2816
EPS = 1e-6

LANES = 128
VMEM_LIMIT_BYTES = 56 * 1024 * 1024

ROW_TILE = 512
ATT_TILE = 256
GLA_TILE = 512
GLA_CHUNK = 128
FF_SPLIT = 2
NEG_INF = float("-inf")
LOG2E = math.log2(math.e)


def _cparams(sem):
    return pltpu.CompilerParams(dimension_semantics=sem, vmem_limit_bytes=VMEM_LIMIT_BYTES)


def _dot(a, b):
    return lax.dot_general(a, b, (((1,), (0,)), ((), ())), preferred_element_type=F32)


def _dot_nt(a, b):
    return lax.dot_general(a, b, (((1,), (1,)), ((), ())), preferred_element_type=F32)


def _split2(a):
    hi = a.astype(BF16)
    lo = (a - hi.astype(F32)).astype(BF16)
    return hi, lo


def _dot_hp(a, b):
    ah, al = _split2(a)
    bh, bl = _split2(b)
    return _dot(ah, bh) + (_dot(ah, bl) + _dot(al, bh))


def _dot_hp_nt(a, b):
    ah, al = _split2(a)
    bh, bl = _split2(b)
    return _dot_nt(ah, bh) + (_dot_nt(ah, bl) + _dot_nt(al, bh))


def _hi_lo_concat(a, order):
    hi = a.astype(BF16).astype(F32)
    parts = (hi, a - hi)
    return jnp.concatenate([parts[o] for o in order], axis=1).astype(BF16)


def _rms(x):
    return x * lax.rsqrt(jnp.mean(x * x, axis=-1, keepdims=True) + EPS)


def _norm_mod(x, g, sc, sh):
    return (_rms(x) * g) * (1.0 + sc) + sh


def _mod_kernel(c_ref, w_ref, b_ref, o_ref):
    c = c_ref[...]
    cond = c * jax.nn.sigmoid(c)
    o_ref[0] = _dot_hp(cond, w_ref[0]) + b_ref[0]


def _modulation(c, ada_w, ada_b):
    depth, d, six_d = ada_w.shape
    b = c.shape[0]
    n_col = six_d // d
    return pl.pallas_call(
        _mod_kernel,
        grid=(depth, n_col),
        in_specs=[
            pl.BlockSpec((b, d), lambda l, j: (0, 0)),
            pl.BlockSpec((1, d, d), lambda l, j: (l, 0, j)),
            pl.BlockSpec((1, 1, d), lambda l, j: (l, 0, j)),
        ],
        out_specs=pl.BlockSpec((1, b, d), lambda l, j: (l, 0, j)),
        out_shape=jax.ShapeDtypeStruct((depth, b, six_d), F32),
        compiler_params=_cparams(("arbitrary", "arbitrary")),
        name="adaln_mod",
    )(c, ada_w, ada_b.reshape(depth, 1, six_d))


def _rms_cols(xt):
    return xt * lax.rsqrt(jnp.mean(xt * xt, axis=0, keepdims=True) + EPS)


def _dsa_pre_kernel(x_ref, mod_ref, g_ref, wk_ref, wkv_ref, wkvt_ref, wqt_ref, gcq_ref, gckv_ref,
                    gckvc_ref, wuqt_ref, wukt_ref, wqit_ref,
                    qlatt_ref, qi3t_ref, ki3_ref, widxt_ref, ckv_ref, ckvt_ref):
    mod = mod_ref[0]
    h = _norm_mod(x_ref[0], g_ref[...], mod[1:2], mod[0:1])
    hb = h.astype(BF16)

    k_idx = _dot_hp(h, wk_ref[...])[:, :IDX_DIM]
    ki3_ref[0] = _hi_lo_concat(k_idx, (0, 1, 0))
    ckv_ref[0] = (_rms(_dot(hb, wkv_ref[...])) * gckv_ref[...]).astype(BF16)
    c_kvt = (_rms_cols(_dot_nt(wkvt_ref[...], hb)) * gckvc_ref[...]).astype(BF16)
    for j in range(ckvt_ref.shape[1]):
        ckvt_ref[0, j] = c_kvt[:, j * ATT_TILE:(j + 1) * ATT_TILE]

    hpt = _dot_hp_nt(wqt_ref[...], h)
    widxt_ref[0] = hpt[Q_RANK:] * (IDX_HEADS ** -0.5 * IDX_DIM ** -0.5)
    c_qt = _rms_cols(hpt[:Q_RANK]) * gcq_ref[...]
    qt = _dot(wuqt_ref[...], c_qt.astype(BF16))
    for hh in range(A_HEADS):
        qh = qt[hh * A_HEAD_DIM:(hh + 1) * A_HEAD_DIM].astype(BF16)
        qlatt_ref[0, hh] = (_dot(wukt_ref[hh], qh) * (A_HEAD_DIM ** -0.5 * LOG2E)).astype(BF16)
    q_idxt = _dot_hp(wqit_ref[...], c_qt)
    for hh in range(IDX_HEADS):
        piece = q_idxt[hh * IDX_DIM:(hh + 1) * IDX_DIM]
        hi = piece.astype(BF16).astype(F32)
        qi3t_ref[0, hh] = jnp.concatenate([hi, hi, piece - hi], axis=0).astype(BF16)


def _dsa_pre(x, mod, g_mix, w_in, g_cq, g_ckv, w_uq, w_uk, w_qi):
    b, t, d = x.shape
    tm = min(ROW_TILE, t)
    w_cq, w_ckv, w_ki, w_wi = jnp.split(w_in, [Q_RANK, Q_RANK + KV_RANK, Q_RANK + KV_RANK + IDX_DIM], axis=1)
    w_k = jnp.concatenate([w_ki, jnp.zeros((d, LANES - IDX_DIM), F32)], axis=1)
    w_qt = jnp.concatenate([w_cq, w_wi], axis=1).T
    const = lambda *shape: pl.BlockSpec(shape, lambda i, j: (0,) * len(shape))
    return pl.pallas_call(
        _dsa_pre_kernel,
        grid=(b, t // tm),
        in_specs=[
            pl.BlockSpec((1, tm, d), lambda i, j: (i, j, 0)),
            pl.BlockSpec((1, 6, d), lambda i, j: (i, 0, 0)),
            const(1, d),
            const(d, LANES),
            const(d, KV_RANK),
            const(KV_RANK, d),
            const(Q_RANK + IDX_HEADS, d),
            const(Q_RANK, 1),
            const(1, KV_RANK),
            const(KV_RANK, 1),
            const(A_HEADS * A_HEAD_DIM, Q_RANK),
            const(A_HEADS, KV_RANK, A_HEAD_DIM),
            const(IDX_HEADS * IDX_DIM, Q_RANK),
        ],
        out_specs=[
            pl.BlockSpec((1, A_HEADS, KV_RANK, tm), lambda i, j: (i, 0, 0, j)),
            pl.BlockSpec((1, IDX_HEADS, 3 * IDX_DIM, tm), lambda i, j: (i, 0, 0, j)),
            pl.BlockSpec((1, tm, 3 * IDX_DIM), lambda i, j: (i, j, 0)),
            pl.BlockSpec((1, IDX_HEADS, tm), lambda i, j: (i, 0, j)),
            pl.BlockSpec((1, tm, KV_RANK), lambda i, j: (i, j, 0)),
            pl.BlockSpec((1, tm // ATT_TILE, KV_RANK, ATT_TILE), lambda i, j: (i, j, 0, 0)),
        ],
        out_shape=[
            jax.ShapeDtypeStruct((b, A_HEADS, KV_RANK, t), BF16),
            jax.ShapeDtypeStruct((b, IDX_HEADS, 3 * IDX_DIM, t), BF16),
            jax.ShapeDtypeStruct((b, t, 3 * IDX_DIM), BF16),
            jax.ShapeDtypeStruct((b, IDX_HEADS, t), F32),
            jax.ShapeDtypeStruct((b, t, KV_RANK), BF16),
            jax.ShapeDtypeStruct((b, t // ATT_TILE, KV_RANK, ATT_TILE), BF16),
        ],
        compiler_params=_cparams(("arbitrary", "arbitrary")),
        name="dsa_pre",
    )(x, mod, g_mix.reshape(1, d), w_k, w_ckv.astype(BF16), w_ckv.T.astype(BF16), w_qt,
      g_cq.reshape(Q_RANK, 1), g_ckv.reshape(1, KV_RANK), g_ckv.reshape(KV_RANK, 1),
      w_uq.T.astype(BF16), jnp.swapaxes(w_uk, 1, 2).astype(BF16), w_qi.T)


def _bucket_starts():
    exact = NUM_BUCKETS // 2
    starts = list(range(1, exact + 1))
    d = exact
    for m in range(exact + 1, NUM_BUCKETS):
        while exact + int(math.log(d / exact) / math.log(MAX_DISTANCE / exact) * (NUM_BUCKETS - exact)) < m:
            d += 1
        starts.append(d)
    return tuple(starts)


BUCKET_STARTS = _bucket_starts()
FAR_DISTANCE = BUCKET_STARTS[-1]


def _bias_kernel(rb_ref, o_ref):
    hh = pl.program_id(0)
    key = lax.broadcasted_iota(I32, (ATT_TILE, ATT_TILE), 0)
    qry = lax.broadcasted_iota(I32, (ATT_TILE, ATT_TILE), 1)
    far = rb_ref[NUM_BUCKETS - 1, hh]
    for kind, offset in ((0, ATT_TILE), (1, 0)):
        dist = jnp.maximum(qry - key + offset, 0)
        acc = jnp.full((ATT_TILE, ATT_TILE), 0.0, F32) + far
        for m in range(NUM_BUCKETS - 2, -1, -1):
            acc = jnp.where(dist < BUCKET_STARTS[m], rb_ref[m, hh], acc)
        o_ref[kind, 0] = (acc - far) * LOG2E


def _bias_tiles(rel_bias):
    assert ATT_TILE >= FAR_DISTANCE
    return pl.pallas_call(
        _bias_kernel,
        grid=(A_HEADS,),
        in_specs=[pl.BlockSpec(memory_space=pltpu.SMEM)],
        out_specs=pl.BlockSpec((2, 1, ATT_TILE, ATT_TILE), lambda h: (0, h, 0, 0)),
        out_shape=jax.ShapeDtypeStruct((2, A_HEADS, ATT_TILE, ATT_TILE), F32),
        compiler_params=_cparams(("arbitrary",)),
        name="t5_bias_tiles",
    )(rel_bias)


def _dsa_attn_kernel(qi3t_ref, ki3_ref, widxt_ref, qlatt_ref, ckv_ref, ckvt_ref, bias_ref, o_ref,
                     s_scr, tie_scr, m_scr, l_scr, acc_scr, *, n_sel, idx_bits):
    qi = pl.program_id(1)
    n_chunks = qi + 1
    tq = ATT_TILE
    q_pos = qi * tq + lax.broadcasted_iota(I32, (1, tq), 1)
    k_loc = lax.broadcasted_iota(I32, (tq, 1), 0)

    w_all = widxt_ref[0]

    def score_chunk(c, carry):
        k3 = ki3_ref[0, pl.ds(pl.multiple_of(c * tq, tq), tq), :]
        sc = jnp.zeros((tq, tq), F32)
        for hh in range(IDX_HEADS):
            sc = sc + w_all[hh:hh + 1, :] * jnp.maximum(_dot(k3, qi3t_ref[0, hh]), 0.0)
        s_scr[c] = jnp.where(c * tq + k_loc <= q_pos, sc, NEG_INF)
        return carry

    lax.fori_loop(0, n_chunks, score_chunk, 0)

    @pl.when((n_chunks & 1) == 1)
    def _():
        s_scr[n_chunks] = jnp.full((tq, tq), NEG_INF, F32)

    def count(pred):
        def body(j, acc):
            c = 2 * j
            ind = (jnp.where(pred(s_scr[c], c * tq + k_loc), 1.0, 0.0).astype(F32)
                   + jnp.where(pred(s_scr[c + 1], (c + 1) * tq + k_loc), 1.0, 0.0).astype(F32))
            return acc + jnp.sum(ind.reshape(tq // 8, 8, tq), axis=0)
        acc = lax.fori_loop(0, lax.shift_right_logical(n_chunks + 1, I32(1)), body, jnp.zeros((8, tq), F32))
        return jnp.sum(acc, axis=0, keepdims=True)

    def key_to_float(key):
        bits = jnp.where(key < 0, key ^ I32(-2 ** 31), ~key)
        return lax.bitcast_convert_type(bits, F32)

    def tau_bit(i, key):
        cand = key | lax.shift_left(I32(1), I32(31) - i)
        cand_f = key_to_float(cand)
        cnt = count(lambda s, kp: s >= cand_f)
        return jnp.where(cnt >= n_sel, cand, key)

    key = lax.fori_loop(0, 32, tau_bit, jnp.zeros((1, tq), I32))
    fewer = (q_pos + 1) < n_sel
    tau = jnp.where(fewer, NEG_INF, key_to_float(key))

    n_gt = count(lambda s, kp: s > tau)
    need = jnp.where(fewer, 4.0 * ki3_ref.shape[1], n_sel - n_gt)

    def tie_bit(i, x):
        cand = x | lax.shift_left(I32(1), I32(idx_bits - 1) - i)
        cnt = count(lambda s, kp: (s == tau) & (kp < cand))
        return jnp.where(cnt < need, cand, x)

    tie_scr[...] = jnp.full(tie_scr.shape, 2 ** idx_bits - 1, I32)
    n_eq = count(lambda s, kp: s == tau)

    @pl.when(jnp.max(n_eq - need) > 0.0)
    def _():
        tie_scr[...] = lax.fori_loop(0, idx_bits, tie_bit, jnp.zeros((1, tq), I32))

    last_tie = tie_scr[...]

    def mask_chunk(c, carry):
        s = s_scr[c]
        k_pos = c * tq + k_loc
        sel = (k_pos <= q_pos) & ((s > tau) | ((s == tau) & (k_pos <= last_tie)))
        s_scr[c] = jnp.where(sel, 0.0, NEG_INF)
        return carry

    lax.fori_loop(0, n_chunks, mask_chunk, 0)

    m_scr[...] = jnp.full(m_scr.shape, NEG_INF, F32)
    l_scr[...] = jnp.zeros(l_scr.shape, F32)
    acc_scr[...] = jnp.zeros(acc_scr.shape, F32)

    def attend(c, bias_kind):
        kv = ckv_ref[0, pl.ds(pl.multiple_of(c * tq, tq), tq), :]
        kvt = ckvt_ref[0, c]
        mask = s_scr[c]
        for hh in range(A_HEADS):
            logit = _dot(kv, qlatt_ref[0, hh]) + mask
            if bias_kind is not None:
                logit = logit + bias_ref[bias_kind, hh]
            m_old = m_scr[hh:hh + 1]
            m_new = jnp.maximum(m_old, jnp.max(logit, axis=0, keepdims=True))
            m_safe = jnp.where(m_new == NEG_INF, 0.0, m_new)
            alpha = jnp.exp2(m_old - m_safe)
            p = jnp.exp2(logit - m_safe)
            l_scr[hh:hh + 1] = alpha * l_scr[hh:hh + 1] + jnp.sum(p, axis=0, keepdims=True)
            acc_scr[hh] = alpha * acc_scr[hh] + _dot(kvt, p.astype(BF16))
            m_scr[hh:hh + 1] = m_new

    def far_chunk(c, carry):
        attend(c, None)
        return carry

    lax.fori_loop(0, qi - 1, far_chunk, 0)

    @pl.when(qi >= 1)
    def _():
        attend(qi - 1, 0)

    attend(qi, 1)
    for hh in range(A_HEADS):
        o_ref[0, hh] = (acc_scr[hh] / l_scr[hh:hh + 1]).astype(BF16)


def _dsa_attn(qi3t, ki3, widxt, qlatt, ckv, ckvt, bias):
    b, t, _ = ckv.shape
    tq = ATT_TILE
    n_sel = min(TOPK_MAX, t // 4)
    assert (t // tq) % 2 == 0
    idx_bits = max(1, (t - 1).bit_length())
    kernel = functools.partial(_dsa_attn_kernel, n_sel=n_sel, idx_bits=idx_bits)
    return pl.pallas_call(
        kernel,
        grid=(b, t // tq),
        in_specs=[
            pl.BlockSpec((1, IDX_HEADS, 3 * IDX_DIM, tq), lambda i, j: (i, 0, 0, j)),
            pl.BlockSpec((1, t, 3 * IDX_DIM), lambda i, j: (i, 0, 0)),
            pl.BlockSpec((1, IDX_HEADS, tq), lambda i, j: (i, 0, j)),
            pl.BlockSpec((1, A_HEADS, KV_RANK, tq), lambda i, j: (i, 0, 0, j)),
            pl.BlockSpec((1, t, KV_RANK), lambda i, j: (i, 0, 0)),
            pl.BlockSpec((1, t // tq, KV_RANK, tq), lambda i, j: (i, 0, 0, 0)),
            pl.BlockSpec((2, A_HEADS, tq, tq), lambda i, j: (0, 0, 0, 0)),
        ],
        out_specs=pl.BlockSpec((1, A_HEADS, KV_RANK, tq), lambda i, j: (i, 0, 0, j)),
        out_shape=jax.ShapeDtypeStruct((b, A_HEADS, KV_RANK, t), BF16),
        scratch_shapes=[
            pltpu.VMEM((t // tq, tq, tq), F32),
            pltpu.VMEM((1, tq), I32),
            pltpu.VMEM((A_HEADS, tq), F32),
            pltpu.VMEM((A_HEADS, tq), F32),
            pltpu.VMEM((A_HEADS, KV_RANK, tq), F32),
        ],
        compiler_params=_cparams(("arbitrary", "arbitrary")),
        name="dsa_attn",
    )(qi3t, ki3, widxt, qlatt, ckv, ckvt, bias)


def _dsa_post_kernel(olatt_ref, wuvt_ref, wo_ref, x_ref, mod_ref, o_ref):
    heads = [_dot(wuvt_ref[hh], olatt_ref[0, hh]).astype(BF16) for hh in range(A_HEADS)]
    ot = jnp.concatenate(heads, axis=0)
    y = lax.dot_general(ot, wo_ref[...], (((0,), (0,)), ((), ())), preferred_element_type=F32)
    o_ref[0] = x_ref[0] + mod_ref[0][2:3] * y


def _dsa_post(olatt, w_uv, w_o, x, mod):
    b, t, d = x.shape
    tm = min(ROW_TILE, t)
    return pl.pallas_call(
        _dsa_post_kernel,
        grid=(b, t // tm),
        in_specs=[
            pl.BlockSpec((1, A_HEADS, KV_RANK, tm), lambda i, j: (i, 0, 0, j)),
            pl.BlockSpec((A_HEADS, A_V_DIM, KV_RANK), lambda i, j: (0, 0, 0)),
            pl.BlockSpec((A_HEADS * A_V_DIM, d), lambda i, j: (0, 0)),
            pl.BlockSpec((1, tm, d), lambda i, j: (i, j, 0)),
            pl.BlockSpec((1, 6, d), lambda i, j: (i, 0, 0)),
        ],
        out_specs=pl.BlockSpec((1, tm, d), lambda i, j: (i, j, 0)),
        out_shape=jax.ShapeDtypeStruct((b, t, d), F32),
        compiler_params=_cparams(("arbitrary", "arbitrary")),
        name="dsa_post",
    )(olatt, jnp.swapaxes(w_uv, 1, 2).astype(BF16), w_o.astype(BF16), x, mod)


CONV_HALO = 8


def _ffn_kernel(x_ref, mod_ref, g_ref, wu_ref, wv_ref, cw_ref, cb_ref, wout_ref, gfin_ref,
                o_ref, h_scr, acc_scr, u_scr, carry_scr, *, final_norm):
    ti = pl.program_id(1)
    f = pl.program_id(2)
    tm = x_ref.shape[1]
    mod = mod_ref[0]

    @pl.when(f == 0)
    def _():
        h_scr[...] = _norm_mod(x_ref[0], g_ref[...], mod[4:5], mod[3:4]).astype(BF16)
        acc_scr[...] = jnp.zeros(acc_scr.shape, F32)

    @pl.when(ti == 0)
    def _():
        carry_scr[f] = jnp.zeros(carry_scr.shape[1:], F32)

    h = h_scr[...]
    u = _dot(h, wu_ref[...])
    v = _dot(h, wv_ref[...])
    u_scr[0:CONV_HALO] = carry_scr[f]
    u_scr[CONV_HALO:CONV_HALO + tm] = u
    carry_scr[f] = u[tm - CONV_HALO:tm]
    cw = cw_ref[...]
    u1 = u_scr[CONV_HALO - 1:CONV_HALO - 1 + tm]
    u2 = u_scr[CONV_HALO - 2:CONV_HALO - 2 + tm]
    uc = cw[0:1] * u2 + cw[1:2] * u1 + cw[2:3] * u + cb_ref[...]
    act = 0.5 * uc * (1.0 + lax.erf(uc * (2.0 ** -0.5)))
    acc_scr[...] += _dot((act * v).astype(BF16), wout_ref[...])

    @pl.when(f == pl.num_programs(2) - 1)
    def _():
        y = x_ref[0] + mod[5:6] * acc_scr[...]
        if final_norm:
            y = _rms(y) * gfin_ref[...]
        o_ref[0] = y


def _conv_ffn(x, mod, g_ffn, w_in, conv_w, conv_b, w_out, g_final, final_norm):
    b, t, d = x.shape
    tm = min(ROW_TILE, t)
    fc = D_FF // FF_SPLIT
    assert fc % LANES == 0
    w_in_b = w_in.astype(BF16)
    kernel = functools.partial(_ffn_kernel, final_norm=final_norm)
    return pl.pallas_call(
        kernel,
        grid=(b, t // tm, FF_SPLIT),
        in_specs=[
            pl.BlockSpec((1, tm, d), lambda i, j, f: (i, j, 0)),
            pl.BlockSpec((1, 6, d), lambda i, j, f: (i, 0, 0)),
            pl.BlockSpec((1, d), lambda i, j, f: (0, 0)),
            pl.BlockSpec((d, fc), lambda i, j, f: (0, f)),
            pl.BlockSpec((d, fc), lambda i, j, f: (0, FF_SPLIT + f)),
            pl.BlockSpec((3, fc), lambda i, j, f: (0, f)),
            pl.BlockSpec((1, fc), lambda i, j, f: (0, f)),
            pl.BlockSpec((fc, d), lambda i, j, f: (f, 0)),
            pl.BlockSpec((1, d), lambda i, j, f: (0, 0)),
        ],
        out_specs=pl.BlockSpec((1, tm, d), lambda i, j, f: (i, j, 0)),
        out_shape=jax.ShapeDtypeStruct((b, t, d), F32),
        scratch_shapes=[
            pltpu.VMEM((tm, d), BF16),
            pltpu.VMEM((tm, d), F32),
            pltpu.VMEM((tm + CONV_HALO, fc), F32),
            pltpu.VMEM((FF_SPLIT, CONV_HALO, fc), F32),
        ],
        compiler_params=_cparams(("arbitrary", "arbitrary", "arbitrary")),
        name="conv_ffn_final" if final_norm else "conv_ffn",
    )(x, mod, g_ffn.reshape(1, d), w_in_b, w_in_b, conv_w, conv_b.reshape(1, D_FF),
      w_out.astype(BF16), g_final.reshape(1, d))


def _gla_pre_kernel(x_ref, mod_ref, g_ref, wqkr_ref, wv_ref, wg_ref, wg2_ref, bg_ref,
                    q_ref, k_ref, r_ref, v_ref, la_ref):
    mod = mod_ref[0]
    h = _norm_mod(x_ref[0], g_ref[...], mod[1:2], mod[0:1])
    hb = h.astype(BF16)
    qkr = _dot(hb, wqkr_ref[...])
    q_ref[0] = (qkr[:, :G_KD] * (G_DK ** -0.5)).astype(BF16)
    k_ref[0] = qkr[:, G_KD:2 * G_KD].astype(BF16)
    r_ref[0] = qkr[:, 2 * G_KD:].astype(BF16)
    v_ref[0] = _dot(hb, wv_ref[...]).astype(BF16)
    g_lr = _dot_hp(h, wg_ref[...])
    gate = _dot_hp(g_lr, wg2_ref[...]) + bg_ref[...]
    log_sig = jnp.minimum(gate, 0.0) - jnp.log1p(jnp.exp(-jnp.abs(gate)))
    la_ref[0] = log_sig / GATE_TAU


def _gla_pre(x, mod, g_mix, w_in, w_g2, b_g):
    b, t, d = x.shape
    tm = min(ROW_TILE, t)
    w_q, w_k, w_v, w_r, w_g = jnp.split(w_in, [G_KD, 2 * G_KD, 2 * G_KD + G_VD, 2 * G_KD + 2 * G_VD], axis=1)
    w_qkr = jnp.concatenate([w_q, w_k, w_r], axis=1).astype(BF16)
    w_g_pad = jnp.concatenate([w_g, jnp.zeros((d, LANES - G_RANK), F32)], axis=1)
    w_g2_pad = jnp.concatenate([w_g2, jnp.zeros((LANES - G_RANK, G_KD), F32)], axis=0)
    const = lambda *shape: pl.BlockSpec(shape, lambda i, j: (0,) * len(shape))
    row = lambda width: pl.BlockSpec((1, tm, width), lambda i, j: (i, j, 0))
    return pl.pallas_call(
        _gla_pre_kernel,
        grid=(b, t // tm),
        in_specs=[
            row(d),
            pl.BlockSpec((1, 6, d), lambda i, j: (i, 0, 0)),
            const(1, d),
            const(d, 2 * G_KD + G_VD),
            const(d, G_VD),
            const(d, LANES),
            const(LANES, G_KD),
            const(1, G_KD),
        ],
        out_specs=[row(G_KD), row(G_KD), row(G_VD), row(G_VD), row(G_KD)],
        out_shape=[
            jax.ShapeDtypeStruct((b, t, G_KD), BF16),
            jax.ShapeDtypeStruct((b, t, G_KD), BF16),
            jax.ShapeDtypeStruct((b, t, G_VD), BF16),
            jax.ShapeDtypeStruct((b, t, G_VD), BF16),
            jax.ShapeDtypeStruct((b, t, G_KD), F32),
        ],
        compiler_params=_cparams(("arbitrary", "arbitrary")),
        name="gla_pre",
    )(x, mod, g_mix.reshape(1, d), w_qkr, w_v.astype(BF16), w_g_pad, w_g2_pad, b_g.reshape(1, G_KD))


GLA_LEVELS = tuple(1 << p for p in range(GLA_CHUNK.bit_length() - 1))


def _gla_kernel(q_ref, k_ref, g_ref, v_ref, o_ref, st_scr):
    n_ch = q_ref.shape[1] // GLA_CHUNK
    cl = GLA_CHUNK

    @pl.when(pl.program_id(2) == 0)
    def _():
        st_scr[...] = jnp.zeros(st_scr.shape, F32)

    def lanes(ref):
        x = ref[0]
        return jnp.concatenate([x[c * cl:(c + 1) * cl] for c in range(n_ch)], axis=1)

    q = lanes(q_ref).astype(F32)
    k = lanes(k_ref).astype(F32)
    g_hi, g_lo = _split2(lanes(g_ref))
    g_pieces = jnp.concatenate([g_hi, g_lo], axis=0)

    r = lax.broadcasted_iota(I32, ((len(GLA_LEVELS) + 1) * cl, 2 * cl), 0)
    cidx = lax.broadcasted_iota(I32, ((len(GLA_LEVELS) + 1) * cl, 2 * cl), 1) & (cl - 1)
    i_loc = r & (cl - 1)
    lvl = lax.shift_right_logical(r, I32(cl.bit_length() - 1))
    half = lax.shift_left(I32(1), lvl)
    ref_row = jnp.where(lvl < len(GLA_LEVELS), (i_loc & ~(2 * half - 1)) | (half - 1), -1)
    sel = (jnp.where(cidx <= i_loc, 1.0, 0.0) - jnp.where(cidx <= ref_row, 1.0, 0.0)).astype(BF16)
    d_all = _dot(sel, g_pieces)

    row = lax.broadcasted_iota(I32, (cl, 1), 0)
    xor = lax.broadcasted_iota(I32, (cl, cl), 0) ^ lax.broadcasted_iota(I32, (cl, cl), 1)

    a = [jnp.where(xor == 0, _dot_nt(q[:, c * G_DK:(c + 1) * G_DK].astype(BF16),
                                     k[:, c * G_DK:(c + 1) * G_DK].astype(BF16)), 0.0)
         for c in range(n_ch)]
    for li, s in enumerate(GLA_LEVELS):
        dl = d_all[li * cl:(li + 1) * cl]
        hi = (row & s) != 0
        e = jnp.exp(jnp.where(hi, dl, -dl))
        qt = jnp.where(hi, q * e, 0.0).astype(BF16)
        kt = jnp.where(hi, 0.0, k * e).astype(BF16)
        for c in range(n_ch):
            p = _dot_nt(qt[:, c * G_DK:(c + 1) * G_DK], kt[:, c * G_DK:(c + 1) * G_DK])
            a[c] = a[c] + jnp.where(xor < 2 * s, p, 0.0)

    b = d_all[len(GLA_LEVELS) * cl:]
    b_last = b[cl - 1:cl]
    q_hat = (q * jnp.exp(b)).astype(BF16)
    k_hat = (k * jnp.exp(b_last - b)).astype(BF16)
    decay = jnp.exp(b_last)

    st = st_scr[...]
    v_all = v_ref[0]
    outs = []
    for c in range(n_ch):
        sl = slice(c * G_DK, (c + 1) * G_DK)
        v_c = v_all[c * cl:(c + 1) * cl]
        o_inter = _dot_nt(q_hat[:, sl], st.astype(BF16))
        o_intra = _dot(a[c].astype(BF16), v_c)
        outs.append(o_inter + o_intra)
        st = st * decay[:, sl] + lax.dot_general(v_c, k_hat[:, sl], (((0,), (0,)), ((), ())),
                                                 preferred_element_type=F32)
    st_scr[...] = st
    o_ref[0] = jnp.concatenate(outs, axis=0).astype(BF16)


def _gla_core(q, k, la, v):
    b, t, _ = q.shape
    tm = min(GLA_TILE, t)
    assert G_DK == LANES and tm % GLA_CHUNK == 0
    return pl.pallas_call(
        _gla_kernel,
        grid=(b, G_HEADS, t // tm),
        in_specs=[
            pl.BlockSpec((1, tm, G_DK), lambda i, h, j: (i, j, h)),
            pl.BlockSpec((1, tm, G_DK), lambda i, h, j: (i, j, h)),
            pl.BlockSpec((1, tm, G_DK), lambda i, h, j: (i, j, h)),
            pl.BlockSpec((1, tm, G_DV), lambda i, h, j: (i, j, h)),
        ],
        out_specs=pl.BlockSpec((1, tm, G_DV), lambda i, h, j: (i, j, h)),
        out_shape=jax.ShapeDtypeStruct((b, t, G_VD), BF16),
        scratch_shapes=[pltpu.VMEM((G_DV, G_DK), F32)],
        compiler_params=_cparams(("arbitrary", "arbitrary", "arbitrary")),
        name="gla_core",
    )(q, k, la, v)


def _gla_post_kernel(o_ref, r_ref, gn_ref, wo_ref, x_ref, mod_ref, out_ref):
    o = o_ref[0].astype(F32)
    gn = gn_ref[...]
    heads = [_rms(o[:, hh * G_DV:(hh + 1) * G_DV]) * gn[:, hh * G_DV:(hh + 1) * G_DV]
             for hh in range(G_HEADS)]
    r = r_ref[0].astype(F32)
    z = jnp.concatenate(heads, axis=1) * (r * jax.nn.sigmoid(r))
    out_ref[0] = x_ref[0] + mod_ref[0][2:3] * _dot(z.astype(BF16), wo_ref[...])


def _gla_post(o, r, g_norm, w_o, x, mod):
    b, t, d = x.shape
    tm = min(ROW_TILE, t)
    row = lambda width: pl.BlockSpec((1, tm, width), lambda i, j: (i, j, 0))
    return pl.pallas_call(
        _gla_post_kernel,
        grid=(b, t // tm),
        in_specs=[row(G_VD), row(G_VD),
                  pl.BlockSpec((1, G_VD), lambda i, j: (0, 0)),
                  pl.BlockSpec((G_VD, d), lambda i, j: (0, 0)),
                  row(d),
                  pl.BlockSpec((1, 6, d), lambda i, j: (i, 0, 0))],
        out_specs=row(d),
        out_shape=jax.ShapeDtypeStruct((b, t, d), F32),
        compiler_params=_cparams(("arbitrary", "arbitrary")),
        name="gla_post",
    )(o, r, g_norm.reshape(1, G_VD), w_o.astype(BF16), x, mod)


def kernel(x, c, rel_bias, a_w_in, a_g_cq, a_g_ckv, a_w_uq, a_w_uk, a_w_uv, a_w_qi, a_w_o, b_w_in, b_w_g2, b_b_g, b_g_norm, b_w_o, ada_w, ada_b, g_mix, g_ffn, f_w_in, f_conv_w, f_conv_b, f_w_out, g_final):
    b, t, d = x.shape
    depth = ada_w.shape[0]
    assert d == D_MODEL and depth == 2 and t % ROW_TILE == 0
    mod = _modulation(c, ada_w, ada_b).reshape(depth, b, 6, d)

    qlatt, qi3t, ki3, widxt, ckv, ckvt = _dsa_pre(x, mod[0], g_mix[0], a_w_in[0], a_g_cq[0], a_g_ckv[0],
                                                  a_w_uq[0], a_w_uk[0], a_w_qi[0])
    olatt = _dsa_attn(qi3t, ki3, widxt, qlatt, ckv, ckvt, _bias_tiles(rel_bias))
    x = _dsa_post(olatt, a_w_uv[0], a_w_o[0], x, mod[0])
    x = _conv_ffn(x, mod[0], g_ffn[0], f_w_in[0], f_conv_w[0], f_conv_b[0], f_w_out[0], g_final, False)

    q, k, r, v, la = _gla_pre(x, mod[1], g_mix[1], b_w_in[0], b_w_g2[0], b_b_g[0])
    o = _gla_core(q, k, la, v)
    x = _gla_post(o, r, b_g_norm[0], b_w_o[0], x, mod[1])
    return _conv_ffn(x, mod[1], g_ffn[1], f_w_in[1], f_conv_w[1], f_conv_b[1], f_w_out[1], g_final, True)
```

```python
import functools
import math

import numpy as np
import jax
import jax.numpy as jnp
from jax import lax
from jax.experimental import pallas as pl
from jax.experimental.pallas import tpu as pltpu

F32 = jnp.float32
BF16 = jnp.bfloat16
I32 = jnp.int32
I16 = jnp.int16

D_MODEL = 1024
A_HEADS = 8
A_HEAD_DIM = 128
A_V_DIM = 128
Q_RANK = 256
KV_RANK = 256
IDX_HEADS = 8
IDX_DIM = 64
TOPK_MAX = 256
NUM_BUCKETS = 32
MAX_DISTANCE = 128
G_HEADS = 4
G_KD = 512
G_VD = 1024
G_DK = G_KD // G_HEADS
G_DV = G_VD // G_HEADS
G_RANK = 16
GATE_TAU = 16.0
D_FF = 2816
EPS = 1e-6

LANES = 128
VMEM_LIMIT_BYTES = 56 * 1024 * 1024

ROW_TILE = 512
ATT_TILE = 256
GLA_TILE = 512
GLA_CHUNK = 128
FF_SPLIT = 2
NEG_INF = float("-inf")
HALF_BIAS = 1 << 15
LOG2E = math.log2(math.e)


def _cparams(sem):
    return pltpu.CompilerParams(dimension_semantics=sem, vmem_limit_bytes=VMEM_LIMIT_BYTES)


def _dot(a, b):
    return lax.dot_general(a, b, (((1,), (0,)), ((), ())), preferred_element_type=F32)


def _dot_nt(a, b):
    return lax.dot_general(a, b, (((1,), (1,)), ((), ())), preferred_element_type=F32)


def _split2(a):
    hi = a.astype(BF16)
    lo = (a - hi.astype(F32)).astype(BF16)
    return hi, lo


def _dot_hp(a, b):
    ah, al = _split2(a)
    bh, bl = _split2(b)
    return _dot(ah, bh) + (_dot(ah, bl) + _dot(al, bh))


def _dot_hp_nt(a, b):
    ah, al = _split2(a)
    bh, bl = _split2(b)
    return _dot_nt(ah, bh) + (_dot_nt(ah, bl) + _dot_nt(al, bh))


def _hi_lo_concat(a, order):
    hi = a.astype(BF16).astype(F32)
    parts = (hi, a - hi)
    return jnp.concatenate([parts[o] for o in order], axis=1).astype(BF16)


def _rms(x):
    return x * lax.rsqrt(jnp.mean(x * x, axis=-1, keepdims=True) + EPS)


def _norm_mod(x, g, sc, sh):
    return (_rms(x) * g) * (1.0 + sc) + sh


def _mod_kernel(c_ref, w_ref, b_ref, o_ref):
    c = c_ref[...]
    cond = c * jax.nn.sigmoid(c)
    o_ref[0] = _dot_hp(cond, w_ref[0]) + b_ref[0]


def _modulation(c, ada_w, ada_b):
    depth, d, six_d = ada_w.shape
    b = c.shape[0]
    n_col = six_d // d
    return pl.pallas_call(
        _mod_kernel,
        grid=(depth, n_col),
        in_specs=[
            pl.BlockSpec((b, d), lambda l, j: (0, 0)),
            pl.BlockSpec((1, d, d), lambda l, j: (l, 0, j)),
            pl.BlockSpec((1, 1, d), lambda l, j: (l, 0, j)),
        ],
        out_specs=pl.BlockSpec((1, b, d), lambda l, j: (l, 0, j)),
        out_shape=jax.ShapeDtypeStruct((depth, b, six_d), F32),
        compiler_params=_cparams(("arbitrary", "arbitrary")),
        name="adaln_mod",
    )(c, ada_w, ada_b.reshape(depth, 1, six_d))


def _rms_cols(xt):
    return xt * lax.rsqrt(jnp.mean(xt * xt, axis=0, keepdims=True) + EPS)


def _dsa_pre_kernel(x_ref, mod_ref, g_ref, wk_ref, wkv_ref, wkvt_ref, wqt_ref, gcq_ref, gckv_ref,
                    gckvc_ref, wuqt_ref, wukt_ref, wqit_ref,
                    qlatt_ref, qi3t_ref, ki3_ref, widxt_ref, ckv_ref, ckvt_ref):
    mod = mod_ref[0]
    h = _norm_mod(x_ref[0], g_ref[...], mod[1:2], mod[0:1])
    hb = h.astype(BF16)

    k_idx = _dot_hp(h, wk_ref[...])[:, :IDX_DIM]
    ki3_ref[0] = _hi_lo_concat(k_idx, (0, 1, 0))
    ckv_ref[0] = (_rms(_dot(hb, wkv_ref[...])) * gckv_ref[...]).astype(BF16)
    c_kvt = (_rms_cols(_dot_nt(wkvt_ref[...], hb)) * gckvc_ref[...]).astype(BF16)
    for j in range(ckvt_ref.shape[1]):
        ckvt_ref[0, j] = c_kvt[:, j * ATT_TILE:(j + 1) * ATT_TILE]

    hpt = _dot_hp_nt(wqt_ref[...], h)
    widxt_ref[0] = hpt[Q_RANK:] * (IDX_HEADS ** -0.5 * IDX_DIM ** -0.5)
    c_qt = _rms_cols(hpt[:Q_RANK]) * gcq_ref[...]
    qt = _dot(wuqt_ref[...], c_qt.astype(BF16))
    for hh in range(A_HEADS):
        qh = qt[hh * A_HEAD_DIM:(hh + 1) * A_HEAD_DIM].astype(BF16)
        qlatt_ref[0, hh] = (_dot(wukt_ref[hh], qh) * (A_HEAD_DIM ** -0.5 * LOG2E)).astype(BF16)
    q_idxt = _dot_hp(wqit_ref[...], c_qt)
    for hh in range(IDX_HEADS):
        piece = q_idxt[hh * IDX_DIM:(hh + 1) * IDX_DIM]
        hi = piece.astype(BF16).astype(F32)
        qi3t_ref[0, hh] = jnp.concatenate([hi, hi, piece - hi], axis=0).astype(BF16)


def _dsa_pre(x, mod, g_mix, w_in, g_cq, g_ckv, w_uq, w_uk, w_qi):
    b, t, d = x.shape
    tm = min(ROW_TILE, t)
    w_cq, w_ckv, w_ki, w_wi = jnp.split(w_in, [Q_RANK, Q_RANK + KV_RANK, Q_RANK + KV_RANK + IDX_DIM], axis=1)
    w_k = jnp.concatenate([w_ki, jnp.zeros((d, LANES - IDX_DIM), F32)], axis=1)
    w_qt = jnp.concatenate([w_cq, w_wi], axis=1).T
    const = lambda *shape: pl.BlockSpec(shape, lambda i, j: (0,) * len(shape))
    return pl.pallas_call(
        _dsa_pre_kernel,
        grid=(b, t // tm),
        in_specs=[
            pl.BlockSpec((1, tm, d), lambda i, j: (i, j, 0)),
            pl.BlockSpec((1, 6, d), lambda i, j: (i, 0, 0)),
            const(1, d),
            const(d, LANES),
            const(d, KV_RANK),
            const(KV_RANK, d),
            const(Q_RANK + IDX_HEADS, d),
            const(Q_RANK, 1),
            const(1, KV_RANK),
            const(KV_RANK, 1),
            const(A_HEADS * A_HEAD_DIM, Q_RANK),
            const(A_HEADS, KV_RANK, A_HEAD_DIM),
            const(IDX_HEADS * IDX_DIM, Q_RANK),
        ],
        out_specs=[
            pl.BlockSpec((1, A_HEADS, KV_RANK, tm), lambda i, j: (i, 0, 0, j)),
            pl.BlockSpec((1, IDX_HEADS, 3 * IDX_DIM, tm), lambda i, j: (i, 0, 0, j)),
            pl.BlockSpec((1, tm, 3 * IDX_DIM), lambda i, j: (i, j, 0)),
            pl.BlockSpec((1, IDX_HEADS, tm), lambda i, j: (i, 0, j)),
            pl.BlockSpec((1, tm, KV_RANK), lambda i, j: (i, j, 0)),
            pl.BlockSpec((1, tm // ATT_TILE, KV_RANK, ATT_TILE), lambda i, j: (i, j, 0, 0)),
        ],
        out_shape=[
            jax.ShapeDtypeStruct((b, A_HEADS, KV_RANK, t), BF16),
            jax.ShapeDtypeStruct((b, IDX_HEADS, 3 * IDX_DIM, t), BF16),
            jax.ShapeDtypeStruct((b, t, 3 * IDX_DIM), BF16),
            jax.ShapeDtypeStruct((b, IDX_HEADS, t), F32),
            jax.ShapeDtypeStruct((b, t, KV_RANK), BF16),
            jax.ShapeDtypeStruct((b, t // ATT_TILE, KV_RANK, ATT_TILE), BF16),
        ],
        compiler_params=_cparams(("arbitrary", "arbitrary")),
        name="dsa_pre",
    )(x, mod, g_mix.reshape(1, d), w_k, w_ckv.astype(BF16), w_ckv.T.astype(BF16), w_qt,
      g_cq.reshape(Q_RANK, 1), g_ckv.reshape(1, KV_RANK), g_ckv.reshape(KV_RANK, 1),
      w_uq.T.astype(BF16), jnp.swapaxes(w_uk, 1, 2).astype(BF16), w_qi.T)


def _bucket_starts():
    exact = NUM_BUCKETS // 2
    starts = list(range(1, exact + 1))
    d = exact
    for m in range(exact + 1, NUM_BUCKETS):
        while exact + int(math.log(d / exact) / math.log(MAX_DISTANCE / exact) * (NUM_BUCKETS - exact)) < m:
            d += 1
        starts.append(d)
    return tuple(starts)


BUCKET_STARTS = _bucket_starts()
FAR_DISTANCE = BUCKET_STARTS[-1]


def _bias_kernel(rb_ref, o_ref):
    hh = pl.program_id(0)
    key = lax.broadcasted_iota(I32, (ATT_TILE, ATT_TILE), 0)
    qry = lax.broadcasted_iota(I32, (ATT_TILE, ATT_TILE), 1)
    far = rb_ref[NUM_BUCKETS - 1, hh]
    for kind, offset in ((0, ATT_TILE), (1, 0)):
        dist = jnp.maximum(qry - key + offset, 0)
        acc = jnp.full((ATT_TILE, ATT_TILE), 0.0, F32) + far
        for m in range(NUM_BUCKETS - 2, -1, -1):
            acc = jnp.where(dist < BUCKET_STARTS[m], rb_ref[m, hh], acc)
        o_ref[kind, 0] = (acc - far) * LOG2E


def _bias_tiles(rel_bias):
    assert ATT_TILE >= FAR_DISTANCE
    return pl.pallas_call(
        _bias_kernel,
        grid=(A_HEADS,),
        in_specs=[pl.BlockSpec(memory_space=pltpu.SMEM)],
        out_specs=pl.BlockSpec((2, 1, ATT_TILE, ATT_TILE), lambda h: (0, h, 0, 0)),
        out_shape=jax.ShapeDtypeStruct((2, A_HEADS, ATT_TILE, ATT_TILE), F32),
        compiler_params=_cparams(("arbitrary",)),
        name="t5_bias_tiles",
    )(rel_bias)


def _dsa_attn_kernel(qi3t_ref, ki3_ref, widxt_ref, qlatt_ref, ckv_ref, ckvt_ref, bias_ref, o_ref,
                     s_scr, hi_scr, lo_scr, tie_scr, m_scr, l_scr, acc_scr, *, n_sel, idx_bits):
    qi = pl.program_id(1)
    n_chunks = qi + 1
    tq = ATT_TILE
    q_pos = qi * tq + lax.broadcasted_iota(I32, (1, tq), 1)
    k_loc = lax.broadcasted_iota(I32, (tq, 1), 0)

    w_all = widxt_ref[0]

    def score_chunk(c, carry):
        k3 = ki3_ref[0, pl.ds(pl.multiple_of(c * tq, tq), tq), :]
        sc = jnp.zeros((tq, tq), F32)
        for hh in range(IDX_HEADS):
            sc = sc + w_all[hh:hh + 1, :] * jnp.maximum(_dot(k3, qi3t_ref[0, hh]), 0.0)
        s_scr[c] = jnp.where(c * tq + k_loc <= q_pos, sc, NEG_INF)
        return carry

    lax.fori_loop(0, n_chunks, score_chunk, 0)

    @pl.when((n_chunks & 1) == 1)
    def _():
        s_scr[n_chunks] = jnp.full((tq, tq), NEG_INF, F32)

    def count(pred):
        def body(j, acc):
            c = 2 * j
            ind = (jnp.where(pred(s_scr[c], c * tq + k_loc), 1.0, 0.0).astype(F32)
                   + jnp.where(pred(s_scr[c + 1], (c + 1) * tq + k_loc), 1.0, 0.0).astype(F32))
            return acc + jnp.sum(ind.reshape(tq // 8, 8, tq), axis=0)
        acc = lax.fori_loop(0, lax.shift_right_logical(n_chunks + 1, I32(1)), body, jnp.zeros((8, tq), F32))
        return jnp.sum(acc, axis=0, keepdims=True)

    def key_to_float(key):
        bits = jnp.where(key < 0, key ^ I32(-2 ** 31), ~key)
        return lax.bitcast_convert_type(bits, F32)

    def split_chunk(c, carry):
        bits = lax.bitcast_convert_type(s_scr[c], I32)
        bits = jnp.where(bits == I32(-2 ** 31), 0, bits)
        ukey = jnp.where(bits < 0, ~bits, bits ^ I32(-2 ** 31))
        hi_scr[c] = (lax.shift_right_logical(ukey, I32(16)) - HALF_BIAS).astype(I16)
        lo_scr[c] = ((ukey & 0xFFFF) - HALF_BIAS).astype(I16)
        return carry

    lax.fori_loop(0, n_chunks, split_chunk, 0)

    @pl.when((n_chunks & 1) == 1)
    def _():
        hi_scr[n_chunks] = jnp.full((tq, tq), -HALF_BIAS, I16)
        lo_scr[n_chunks] = jnp.full((tq, tq), -HALF_BIAS, I16)

    def count16(src, pred):
        def body(j, acc):
            c = 2 * j
            ind = (jnp.where(pred(src[c]), I16(1), I16(0)) + jnp.where(pred(src[c + 1]), I16(1), I16(0)))
            rows = ind.reshape(tq // 16, 16, tq)
            part = rows[0]
            for i in range(1, tq // 16):
                part = part + rows[i]
            return acc + part
        acc = lax.fori_loop(0, lax.shift_right_logical(n_chunks + 1, I32(1)), body,
                            jnp.zeros((16, tq), I16))
        return jnp.sum(acc.astype(I32), axis=0, keepdims=True)

    def half_search(src, target):
        def bit(i, val):
            cand = val | lax.shift_left(I32(1), I32(15) - i)
            cand16 = (cand - HALF_BIAS).astype(I16)
            return jnp.where(count16(src, lambda t: t >= cand16) >= target, cand, val)
        return lax.fori_loop(0, 16, bit, jnp.zeros((1, tq), I32))

    key_hi = half_search(hi_scr, n_sel)
    hi16 = (key_hi - HALF_BIAS).astype(I16)
    n_above = count16(hi_scr, lambda t: t > hi16)

    def keep_matching(c, carry):
        lo_scr[c] = jnp.where(hi_scr[c] == hi16, lo_scr[c], I16(-HALF_BIAS))
        return carry

    lax.fori_loop(0, n_chunks, keep_matching, 0)
    key_lo = half_search(lo_scr, n_sel - n_above)
    key = lax.shift_left(key_hi, I32(16)) | key_lo
    fewer = (q_pos + 1) < n_sel
    tau = jnp.where(fewer, NEG_INF, key_to_float(key))

    n_gt = count(lambda s, kp: s > tau)
    need = jnp.where(fewer, 4.0 * ki3_ref.shape[1], n_sel - n_gt)

    def tie_bit(i, x):
        cand = x | lax.shift_left(I32(1), I32(idx_bits - 1) - i)
        cnt = count(lambda s, kp: (s == tau) & (kp < cand))
        return jnp.where(cnt < need, cand, x)

    tie_scr[...] = jnp.full(tie_scr.shape, 2 ** idx_bits - 1, I32)
    n_eq = count(lambda s, kp: s == tau)

    @pl.when(jnp.max(n_eq - need) > 0.0)
    def _():
        tie_scr[...] = lax.fori_loop(0, idx_bits, tie_bit, jnp.zeros((1, tq), I32))

    last_tie = tie_scr[...]

    def mask_chunk(c, carry):
        s = s_scr[c]
        k_pos = c * tq + k_loc
        sel = (k_pos <= q_pos) & ((s > tau) | ((s == tau) & (k_pos <= last_tie)))
        s_scr[c] = jnp.where(sel, 0.0, NEG_INF)
        return carry

    lax.fori_loop(0, n_chunks, mask_chunk, 0)

    m_scr[...] = jnp.full(m_scr.shape, NEG_INF, F32)
    l_scr[...] = jnp.zeros(l_scr.shape, F32)
    acc_scr[...] = jnp.zeros(acc_scr.shape, F32)

    def attend(c, bias_kind):
        kv = ckv_ref[0, pl.ds(pl.multiple_of(c * tq, tq), tq), :]
        kvt = ckvt_ref[0, c]
        mask = s_scr[c]
        for hh in range(A_HEADS):
            logit = _dot(kv, qlatt_ref[0, hh]) + mask
            if bias_kind is not None:
                logit = logit + bias_ref[bias_kind, hh]
            m_old = m_scr[hh:hh + 1]
            m_new = jnp.maximum(m_old, jnp.max(logit, axis=0, keepdims=True))
            m_safe = jnp.where(m_new == NEG_INF, 0.0, m_new)
            alpha = jnp.exp2(m_old - m_safe)
            p = jnp.exp2(logit - m_safe)
            l_scr[hh:hh + 1] = alpha * l_scr[hh:hh + 1] + jnp.sum(p, axis=0, keepdims=True)
            acc_scr[hh] = alpha * acc_scr[hh] + _dot(kvt, p.astype(BF16))
            m_scr[hh:hh + 1] = m_new

    def far_chunk(c, carry):
        attend(c, None)
        return carry

    lax.fori_loop(0, qi - 1, far_chunk, 0)

    @pl.when(qi >= 1)
    def _():
        attend(qi - 1, 0)

    attend(qi, 1)
    for hh in range(A_HEADS):
        o_ref[0, hh] = (acc_scr[hh] / l_scr[hh:hh + 1]).astype(BF16)


def _dsa_attn(qi3t, ki3, widxt, qlatt, ckv, ckvt, bias):
    b, t, _ = ckv.shape
    tq = ATT_TILE
    n_sel = min(TOPK_MAX, t // 4)
    assert (t // tq) % 2 == 0
    idx_bits = max(1, (t - 1).bit_length())
    kernel = functools.partial(_dsa_attn_kernel, n_sel=n_sel, idx_bits=idx_bits)
    return pl.pallas_call(
        kernel,
        grid=(b, t // tq),
        in_specs=[
            pl.BlockSpec((1, IDX_HEADS, 3 * IDX_DIM, tq), lambda i, j: (i, 0, 0, j)),
            pl.BlockSpec((1, t, 3 * IDX_DIM), lambda i, j: (i, 0, 0)),
            pl.BlockSpec((1, IDX_HEADS, tq), lambda i, j: (i, 0, j)),
            pl.BlockSpec((1, A_HEADS, KV_RANK, tq), lambda i, j: (i, 0, 0, j)),
            pl.BlockSpec((1, t, KV_RANK), lambda i, j: (i, 0, 0)),
            pl.BlockSpec((1, t // tq, KV_RANK, tq), lambda i, j: (i, 0, 0, 0)),
            pl.BlockSpec((2, A_HEADS, tq, tq), lambda i, j: (0, 0, 0, 0)),
        ],
        out_specs=pl.BlockSpec((1, A_HEADS, KV_RANK, tq), lambda i, j: (i, 0, 0, j)),
        out_shape=jax.ShapeDtypeStruct((b, A_HEADS, KV_RANK, t), BF16),
        scratch_shapes=[
            pltpu.VMEM((t // tq, tq, tq), F32),
            pltpu.VMEM((t // tq, tq, tq), I16),
            pltpu.VMEM((t // tq, tq, tq), I16),
            pltpu.VMEM((1, tq), I32),
            pltpu.VMEM((A_HEADS, tq), F32),
            pltpu.VMEM((A_HEADS, tq), F32),
            pltpu.VMEM((A_HEADS, KV_RANK, tq), F32),
        ],
        compiler_params=_cparams(("arbitrary", "arbitrary")),
        name="dsa_attn",
    )(qi3t, ki3, widxt, qlatt, ckv, ckvt, bias)


def _dsa_post_kernel(olatt_ref, wuvt_ref, wo_ref, x_ref, mod_ref, o_ref):
    heads = [_dot(wuvt_ref[hh], olatt_ref[0, hh]).astype(BF16) for hh in range(A_HEADS)]
    ot = jnp.concatenate(heads, axis=0)
    y = lax.dot_general(ot, wo_ref[...], (((0,), (0,)), ((), ())), preferred_element_type=F32)
    o_ref[0] = x_ref[0] + mod_ref[0][2:3] * y


def _dsa_post(olatt, w_uv, w_o, x, mod):
    b, t, d = x.shape
    tm = min(ROW_TILE, t)
    return pl.pallas_call(
        _dsa_post_kernel,
        grid=(b, t // tm),
        in_specs=[
            pl.BlockSpec((1, A_HEADS, KV_RANK, tm), lambda i, j: (i, 0, 0, j)),
            pl.BlockSpec((A_HEADS, A_V_DIM, KV_RANK), lambda i, j: (0, 0, 0)),
            pl.BlockSpec((A_HEADS * A_V_DIM, d), lambda i, j: (0, 0)),
            pl.BlockSpec((1, tm, d), lambda i, j: (i, j, 0)),
            pl.BlockSpec((1, 6, d), lambda i, j: (i, 0, 0)),
        ],
        out_specs=pl.BlockSpec((1, tm, d), lambda i, j: (i, j, 0)),
        out_shape=jax.ShapeDtypeStruct((b, t, d), F32),
        compiler_params=_cparams(("arbitrary", "arbitrary")),
        name="dsa_post",
    )(olatt, jnp.swapaxes(w_uv, 1, 2).astype(BF16), w_o.astype(BF16), x, mod)


CONV_HALO = 8


def _ffn_kernel(x_ref, mod_ref, g_ref, wu_ref, wv_ref, cw_ref, cb_ref, wout_ref, gfin_ref,
                o_ref, h_scr, acc_scr, u_scr, carry_scr, *, final_norm):
    ti = pl.program_id(1)
    f = pl.program_id(2)
    tm = x_ref.shape[1]
    mod = mod_ref[0]

    @pl.when(f == 0)
    def _():
        h_scr[...] = _norm_mod(x_ref[0], g_ref[...], mod[4:5], mod[3:4]).astype(BF16)
        acc_scr[...] = jnp.zeros(acc_scr.shape, F32)

    @pl.when(ti == 0)
    def _():
        carry_scr[f] = jnp.zeros(carry_scr.shape[1:], F32)

    h = h_scr[...]
    u = _dot(h, wu_ref[...])
    v = _dot(h, wv_ref[...])
    u_scr[0:CONV_HALO] = carry_scr[f]
    u_scr[CONV_HALO:CONV_HALO + tm] = u
    carry_scr[f] = u[tm - CONV_HALO:tm]
    cw = cw_ref[...]
    u1 = u_scr[CONV_HALO - 1:CONV_HALO - 1 + tm]
    u2 = u_scr[CONV_HALO - 2:CONV_HALO - 2 + tm]
    uc = cw[0:1] * u2 + cw[1:2] * u1 + cw[2:3] * u + cb_ref[...]
    act = 0.5 * uc * (1.0 + lax.erf(uc * (2.0 ** -0.5)))
    acc_scr[...] += _dot((act * v).astype(BF16), wout_ref[...])

    @pl.when(f == pl.num_programs(2) - 1)
    def _():
        y = x_ref[0] + mod[5:6] * acc_scr[...]
        if final_norm:
            y = _rms(y) * gfin_ref[...]
        o_ref[0] = y


def _conv_ffn(x, mod, g_ffn, w_in, conv_w, conv_b, w_out, g_final, final_norm):
    b, t, d = x.shape
    tm = min(ROW_TILE, t)
    fc = D_FF // FF_SPLIT
    assert fc % LANES == 0
    w_in_b = w_in.astype(BF16)
    kernel = functools.partial(_ffn_kernel, final_norm=final_norm)
    return pl.pallas_call(
        kernel,
        grid=(b, t // tm, FF_SPLIT),
        in_specs=[
            pl.BlockSpec((1, tm, d), lambda i, j, f: (i, j, 0)),
            pl.BlockSpec((1, 6, d), lambda i, j, f: (i, 0, 0)),
            pl.BlockSpec((1, d), lambda i, j, f: (0, 0)),
            pl.BlockSpec((d, fc), lambda i, j, f: (0, f)),
            pl.BlockSpec((d, fc), lambda i, j, f: (0, FF_SPLIT + f)),
            pl.BlockSpec((3, fc), lambda i, j, f: (0, f)),
            pl.BlockSpec((1, fc), lambda i, j, f: (0, f)),
            pl.BlockSpec((fc, d), lambda i, j, f: (f, 0)),
            pl.BlockSpec((1, d), lambda i, j, f: (0, 0)),
        ],
        out_specs=pl.BlockSpec((1, tm, d), lambda i, j, f: (i, j, 0)),
        out_shape=jax.ShapeDtypeStruct((b, t, d), F32),
        scratch_shapes=[
            pltpu.VMEM((tm, d), BF16),
            pltpu.VMEM((tm, d), F32),
            pltpu.VMEM((tm + CONV_HALO, fc), F32),
            pltpu.VMEM((FF_SPLIT, CONV_HALO, fc), F32),
        ],
        compiler_params=_cparams(("arbitrary", "arbitrary", "arbitrary")),
        name="conv_ffn_final" if final_norm else "conv_ffn",
    )(x, mod, g_ffn.reshape(1, d), w_in_b, w_in_b, conv_w, conv_b.reshape(1, D_FF),
      w_out.astype(BF16), g_final.reshape(1, d))


def _gla_pre_kernel(x_ref, mod_ref, g_ref, wqkr_ref, wv_ref, wg_ref, wg2_ref, bg_ref,
                    q_ref, k_ref, r_ref, v_ref, la_ref):
    mod = mod_ref[0]
    h = _norm_mod(x_ref[0], g_ref[...], mod[1:2], mod[0:1])
    hb = h.astype(BF16)
    qkr = _dot(hb, wqkr_ref[...])
    q_ref[0] = (qkr[:, :G_KD] * (G_DK ** -0.5)).astype(BF16)
    k_ref[0] = qkr[:, G_KD:2 * G_KD].astype(BF16)
    r_ref[0] = qkr[:, 2 * G_KD:].astype(BF16)
    v_ref[0] = _dot(hb, wv_ref[...]).astype(BF16)
    g_lr = _dot_hp(h, wg_ref[...])
    gate = _dot_hp(g_lr, wg2_ref[...]) + bg_ref[...]
    log_sig = jnp.minimum(gate, 0.0) - jnp.log1p(jnp.exp(-jnp.abs(gate)))
    la_ref[0] = log_sig / GATE_TAU


def _gla_pre(x, mod, g_mix, w_in, w_g2, b_g):
    b, t, d = x.shape
    tm = min(ROW_TILE, t)
    w_q, w_k, w_v, w_r, w_g = jnp.split(w_in, [G_KD, 2 * G_KD, 2 * G_KD + G_VD, 2 * G_KD + 2 * G_VD], axis=1)
    w_qkr = jnp.concatenate([w_q, w_k, w_r], axis=1).astype(BF16)
    w_g_pad = jnp.concatenate([w_g, jnp.zeros((d, LANES - G_RANK), F32)], axis=1)
    w_g2_pad = jnp.concatenate([w_g2, jnp.zeros((LANES - G_RANK, G_KD), F32)], axis=0)
    const = lambda *shape: pl.BlockSpec(shape, lambda i, j: (0,) * len(shape))
    row = lambda width: pl.BlockSpec((1, tm, width), lambda i, j: (i, j, 0))
    return pl.pallas_call(
        _gla_pre_kernel,
        grid=(b, t // tm),
        in_specs=[
            row(d),
            pl.BlockSpec((1, 6, d), lambda i, j: (i, 0, 0)),
            const(1, d),
            const(d, 2 * G_KD + G_VD),
            const(d, G_VD),
            const(d, LANES),
            const(LANES, G_KD),
            const(1, G_KD),
        ],
        out_specs=[row(G_KD), row(G_KD), row(G_VD), row(G_VD), row(G_KD)],
        out_shape=[
            jax.ShapeDtypeStruct((b, t, G_KD), BF16),
            jax.ShapeDtypeStruct((b, t, G_KD), BF16),
            jax.ShapeDtypeStruct((b, t, G_VD), BF16),
            jax.ShapeDtypeStruct((b, t, G_VD), BF16),
            jax.ShapeDtypeStruct((b, t, G_KD), F32),
        ],
        compiler_params=_cparams(("arbitrary", "arbitrary")),
        name="gla_pre",
    )(x, mod, g_mix.reshape(1, d), w_qkr, w_v.astype(BF16), w_g_pad, w_g2_pad, b_g.reshape(1, G_KD))


GLA_LEVELS = tuple(1 << p for p in range(GLA_CHUNK.bit_length() - 1))


def _gla_kernel(q_ref, k_ref, g_ref, v_ref, o_ref, st_scr):
    n_ch = q_ref.shape[1] // GLA_CHUNK
    cl = GLA_CHUNK

    @pl.when(pl.program_id(2) == 0)
    def _():
        st_scr[...] = jnp.zeros(st_scr.shape, F32)

    def lanes(ref):
        x = ref[0]
        return jnp.concatenate([x[c * cl:(c + 1) * cl] for c in range(n_ch)], axis=1)

    q = lanes(q_ref).astype(F32)
    k = lanes(k_ref).astype(F32)
    g_hi, g_lo = _split2(lanes(g_ref))
    g_pieces = jnp.concatenate([g_hi, g_lo], axis=0)

    r = lax.broadcasted_iota(I32, ((len(GLA_LEVELS) + 1) * cl, 2 * cl), 0)
    cidx = lax.broadcasted_iota(I32, ((len(GLA_LEVELS) + 1) * cl, 2 * cl), 1) & (cl - 1)
    i_loc = r & (cl - 1)
    lvl = lax.shift_right_logical(r, I32(cl.bit_length() - 1))
    half = lax.shift_left(I32(1), lvl)
    ref_row = jnp.where(lvl < len(GLA_LEVELS), (i_loc & ~(2 * half - 1)) | (half - 1), -1)
    sel = (jnp.where(cidx <= i_loc, 1.0, 0.0) - jnp.where(cidx <= ref_row, 1.0, 0.0)).astype(BF16)
    d_all = _dot(sel, g_pieces)

    row = lax.broadcasted_iota(I32, (cl, 1), 0)
    xor = lax.broadcasted_iota(I32, (cl, cl), 0) ^ lax.broadcasted_iota(I32, (cl, cl), 1)

    a = [jnp.where(xor == 0, _dot_nt(q[:, c * G_DK:(c + 1) * G_DK].astype(BF16),
                                     k[:, c * G_DK:(c + 1) * G_DK].astype(BF16)), 0.0)
         for c in range(n_ch)]
    for li, s in enumerate(GLA_LEVELS):
        dl = d_all[li * cl:(li + 1) * cl]
        hi = (row & s) != 0
        e = jnp.exp(jnp.where(hi, dl, -dl))
        qt = jnp.where(hi, q * e, 0.0).astype(BF16)
        kt = jnp.where(hi, 0.0, k * e).astype(BF16)
        for c in range(n_ch):
            p = _dot_nt(qt[:, c * G_DK:(c + 1) * G_DK], kt[:, c * G_DK:(c + 1) * G_DK])
            a[c] = a[c] + jnp.where(xor < 2 * s, p, 0.0)

    b = d_all[len(GLA_LEVELS) * cl:]
    b_last = b[cl - 1:cl]
    q_hat = (q * jnp.exp(b)).astype(BF16)
    k_hat = (k * jnp.exp(b_last - b)).astype(BF16)
    decay = jnp.exp(b_last)

    st = st_scr[...]
    v_all = v_ref[0]
    outs = []
    for c in range(n_ch):
        sl = slice(c * G_DK, (c + 1) * G_DK)
        v_c = v_all[c * cl:(c + 1) * cl]
        o_inter = _dot_nt(q_hat[:, sl], st.astype(BF16))
        o_intra = _dot(a[c].astype(BF16), v_c)
        outs.append(o_inter + o_intra)
        st = st * decay[:, sl] + lax.dot_general(v_c, k_hat[:, sl], (((0,), (0,)), ((), ())),
                                                 preferred_element_type=F32)
    st_scr[...] = st
    o_ref[0] = jnp.concatenate(outs, axis=0).astype(BF16)


def _gla_core(q, k, la, v):
    b, t, _ = q.shape
    tm = min(GLA_TILE, t)
    assert G_DK == LANES and tm % GLA_CHUNK == 0
    return pl.pallas_call(
        _gla_kernel,
        grid=(b, G_HEADS, t // tm),
        in_specs=[
            pl.BlockSpec((1, tm, G_DK), lambda i, h, j: (i, j, h)),
            pl.BlockSpec((1, tm, G_DK), lambda i, h, j: (i, j, h)),
            pl.BlockSpec((1, tm, G_DK), lambda i, h, j: (i, j, h)),
            pl.BlockSpec((1, tm, G_DV), lambda i, h, j: (i, j, h)),
        ],
        out_specs=pl.BlockSpec((1, tm, G_DV), lambda i, h, j: (i, j, h)),
        out_shape=jax.ShapeDtypeStruct((b, t, G_VD), BF16),
        scratch_shapes=[pltpu.VMEM((G_DV, G_DK), F32)],
        compiler_params=_cparams(("arbitrary", "arbitrary", "arbitrary")),
        name="gla_core",
    )(q, k, la, v)


def _gla_post_kernel(o_ref, r_ref, gn_ref, wo_ref, x_ref, mod_ref, out_ref):
    o = o_ref[0].astype(F32)
    gn = gn_ref[...]
    heads = [_rms(o[:, hh * G_DV:(hh + 1) * G_DV]) * gn[:, hh * G_DV:(hh + 1) * G_DV]
             for hh in range(G_HEADS)]
    r = r_ref[0].astype(F32)
    z = jnp.concatenate(heads, axis=1) * (r * jax.nn.sigmoid(r))
    out_ref[0] = x_ref[0] + mod_ref[0][2:3] * _dot(z.astype(BF16), wo_ref[...])


def _gla_post(o, r, g_norm, w_o, x, mod):
    b, t, d = x.shape
    tm = min(ROW_TILE, t)
    row = lambda width: pl.BlockSpec((1, tm, width), lambda i, j: (i, j, 0))
    return pl.pallas_call(
        _gla_post_kernel,
        grid=(b, t // tm),
        in_specs=[row(G_VD), row(G_VD),
                  pl.BlockSpec((1, G_VD), lambda i, j: (0, 0)),
                  pl.BlockSpec((G_VD, d), lambda i, j: (0, 0)),
                  row(d),
                  pl.BlockSpec((1, 6, d), lambda i, j: (i, 0, 0))],
        out_specs=row(d),
        out_shape=jax.ShapeDtypeStruct((b, t, d), F32),
        compiler_params=_cparams(("arbitrary", "arbitrary")),
        name="gla_post",
    )(o, r, g_norm.reshape(1, G_VD), w_o.astype(BF16), x, mod)


def kernel(x, c, rel_bias, a_w_in, a_g_cq, a_g_ckv, a_w_uq, a_w_uk, a_w_uv, a_w_qi, a_w_o, b_w_in, b_w_g2, b_b_g, b_g_norm, b_w_o, ada_w, ada_b, g_mix, g_ffn, f_w_in, f_conv_w, f_conv_b, f_w_out, g_final):
    b, t, d = x.shape
    depth = ada_w.shape[0]
    assert d == D_MODEL and depth == 2 and t % ROW_TILE == 0
    mod = _modulation(c, ada_w, ada_b).reshape(depth, b, 6, d)

    qlatt, qi3t, ki3, widxt, ckv, ckvt = _dsa_pre(x, mod[0], g_mix[0], a_w_in[0], a_g_cq[0], a_g_ckv[0],
                                                  a_w_uq[0], a_w_uk[0], a_w_qi[0])
    olatt = _dsa_attn(qi3t, ki3, widxt, qlatt, ckv, ckvt, _bias_tiles(rel_bias))
    x = _dsa_post(olatt, a_w_uv[0], a_w_o[0], x, mod[0])
    x = _conv_ffn(x, mod[0], g_ffn[0], f_w_in[0], f_conv_w[0], f_conv_b[0], f_w_out[0], g_final, False)

    q, k, r, v, la = _gla_pre(x, mod[1], g_mix[1], b_w_in[0], b_w_g2[0], b_b_g[0])
    o = _gla_core(q, k, la, v)
    x = _gla_post(o, r, b_g_norm[0], b_w_o[0], x, mod[1])
    return _conv_ffn(x, mod[1], g_ffn[1], f_w_in[1], f_conv_w[1], f_conv_b[1], f_w_out[1], g_final, True)
```

```python
import functools
import math

import numpy as np
import jax
import jax.numpy as jnp
from jax import lax
from jax.experimental import pallas as pl
from jax.experimental.pallas import tpu as pltpu

F32 = jnp.float32
BF16 = jnp.bfloat16
I32 = jnp.int32
I16 = jnp.int16

D_MODEL = 1024
A_HEADS = 8
A_HEAD_DIM = 128
A_V_DIM = 128
Q_RANK = 256
KV_RANK = 256
IDX_HEADS = 8
IDX_DIM = 64
TOPK_MAX = 256
NUM_BUCKETS = 32
MAX_DISTANCE = 128
G_HEADS = 4
G_KD = 512
G_VD = 1024
G_DK = G_KD // G_HEADS
G_DV = G_VD // G_HEADS
G_RANK = 16
GATE_TAU = 16.0
D_FF = 2816
EPS = 1e-6

LANES = 128
VMEM_LIMIT_BYTES = 56 * 1024 * 1024

ROW_TILE = 512
ATT_TILE = 256
GLA_TILE = 512
GLA_CHUNK = 128
FF_SPLIT = 2
NEG_INF = float("-inf")
HALF_BIAS = 1 << 15
LOG2E = math.log2(math.e)


def _cparams(sem):
    return pltpu.CompilerParams(dimension_semantics=sem, vmem_limit_bytes=VMEM_LIMIT_BYTES)


def _dot(a, b):
    return lax.dot_general(a, b, (((1,), (0,)), ((), ())), preferred_element_type=F32)


def _dot_nt(a, b):
    return lax.dot_general(a, b, (((1,), (1,)), ((), ())), preferred_element_type=F32)


def _split2(a):
    hi = a.astype(BF16)
    lo = (a - hi.astype(F32)).astype(BF16)
    return hi, lo


def _dot_hp(a, b):
    ah, al = _split2(a)
    bh, bl = _split2(b)
    return _dot(ah, bh) + (_dot(ah, bl) + _dot(al, bh))


def _dot_hp_nt(a, b):
    ah, al = _split2(a)
    bh, bl = _split2(b)
    return _dot_nt(ah, bh) + (_dot_nt(ah, bl) + _dot_nt(al, bh))


def _hi_lo_concat(a, order):
    hi = a.astype(BF16).astype(F32)
    parts = (hi, a - hi)
    return jnp.concatenate([parts[o] for o in order], axis=1).astype(BF16)


def _rms(x):
    return x * lax.rsqrt(jnp.mean(x * x, axis=-1, keepdims=True) + EPS)


def _norm_mod(x, g, sc, sh):
    return (_rms(x) * g) * (1.0 + sc) + sh


def _mod_kernel(c_ref, w_ref, b_ref, o_ref):
    c = c_ref[...]
    cond = c * jax.nn.sigmoid(c)
    o_ref[0] = _dot_hp(cond, w_ref[0]) + b_ref[0]


def _modulation(c, ada_w, ada_b):
    depth, d, six_d = ada_w.shape
    b = c.shape[0]
    n_col = six_d // d
    return pl.pallas_call(
        _mod_kernel,
        grid=(depth, n_col),
        in_specs=[
            pl.BlockSpec((b, d), lambda l, j: (0, 0)),
            pl.BlockSpec((1, d, d), lambda l, j: (l, 0, j)),
            pl.BlockSpec((1, 1, d), lambda l, j: (l, 0, j)),
        ],
        out_specs=pl.BlockSpec((1, b, d), lambda l, j: (l, 0, j)),
        out_shape=jax.ShapeDtypeStruct((depth, b, six_d), F32),
        compiler_params=_cparams(("arbitrary", "arbitrary")),
        name="adaln_mod",
    )(c, ada_w, ada_b.reshape(depth, 1, six_d))


def _rms_cols(xt):
    return xt * lax.rsqrt(jnp.mean(xt * xt, axis=0, keepdims=True) + EPS)


def _dsa_pre_kernel(x_ref, mod_ref, g_ref, wk_ref, wkv_ref, wkvt_ref, wqt_ref, gcq_ref, gckv_ref,
                    gckvc_ref, wuqt_ref, wukt_ref, wqit_ref,
                    qlatt_ref, qi3t_ref, ki3_ref, widxt_ref, ckv_ref, ckvt_ref):
    mod = mod_ref[0]
    h = _norm_mod(x_ref[0], g_ref[...], mod[1:2], mod[0:1])
    hb = h.astype(BF16)

    k_idx = _dot_hp(h, wk_ref[...])[:, :IDX_DIM]
    ki3_ref[0] = _hi_lo_concat(k_idx, (0, 1, 0))
    ckv_ref[0] = (_rms(_dot(hb, wkv_ref[...])) * gckv_ref[...]).astype(BF16)
    c_kvt = (_rms_cols(_dot_nt(wkvt_ref[...], hb)) * gckvc_ref[...]).astype(BF16)
    for j in range(ckvt_ref.shape[1]):
        ckvt_ref[0, j] = c_kvt[:, j * ATT_TILE:(j + 1) * ATT_TILE]

    hpt = _dot_hp_nt(wqt_ref[...], h)
    widxt_ref[0] = hpt[Q_RANK:] * (IDX_HEADS ** -0.5 * IDX_DIM ** -0.5)
    c_qt = _rms_cols(hpt[:Q_RANK]) * gcq_ref[...]
    qt = _dot(wuqt_ref[...], c_qt.astype(BF16))
    for hh in range(A_HEADS):
        qh = qt[hh * A_HEAD_DIM:(hh + 1) * A_HEAD_DIM].astype(BF16)
        qlatt_ref[0, hh] = (_dot(wukt_ref[hh], qh) * (A_HEAD_DIM ** -0.5 * LOG2E)).astype(BF16)
    q_idxt = _dot_hp(wqit_ref[...], c_qt)
    for hh in range(IDX_HEADS):
        piece = q_idxt[hh * IDX_DIM:(hh + 1) * IDX_DIM]
        hi = piece.astype(BF16).astype(F32)
        qi3t_ref[0, hh] = jnp.concatenate([hi, hi, piece - hi], axis=0).astype(BF16)


def _dsa_pre(x, mod, g_mix, w_in, g_cq, g_ckv, w_uq, w_uk, w_qi):
    b, t, d = x.shape
    tm = min(ROW_TILE, t)
    w_cq, w_ckv, w_ki, w_wi = jnp.split(w_in, [Q_RANK, Q_RANK + KV_RANK, Q_RANK + KV_RANK + IDX_DIM], axis=1)
    w_k = jnp.concatenate([w_ki, jnp.zeros((d, LANES - IDX_DIM), F32)], axis=1)
    w_qt = jnp.concatenate([w_cq, w_wi], axis=1).T
    const = lambda *shape: pl.BlockSpec(shape, lambda i, j: (0,) * len(shape))
    return pl.pallas_call(
        _dsa_pre_kernel,
        grid=(b, t // tm),
        in_specs=[
            pl.BlockSpec((1, tm, d), lambda i, j: (i, j, 0)),
            pl.BlockSpec((1, 6, d), lambda i, j: (i, 0, 0)),
            const(1, d),
            const(d, LANES),
            const(d, KV_RANK),
            const(KV_RANK, d),
            const(Q_RANK + IDX_HEADS, d),
            const(Q_RANK, 1),
            const(1, KV_RANK),
            const(KV_RANK, 1),
            const(A_HEADS * A_HEAD_DIM, Q_RANK),
            const(A_HEADS, KV_RANK, A_HEAD_DIM),
            const(IDX_HEADS * IDX_DIM, Q_RANK),
        ],
        out_specs=[
            pl.BlockSpec((1, A_HEADS, KV_RANK, tm), lambda i, j: (i, 0, 0, j)),
            pl.BlockSpec((1, IDX_HEADS, 3 * IDX_DIM, tm), lambda i, j: (i, 0, 0, j)),
            pl.BlockSpec((1, tm, 3 * IDX_DIM), lambda i, j: (i, j, 0)),
            pl.BlockSpec((1, IDX_HEADS, tm), lambda i, j: (i, 0, j)),
            pl.BlockSpec((1, tm, KV_RANK), lambda i, j: (i, j, 0)),
            pl.BlockSpec((1, tm // ATT_TILE, KV_RANK, ATT_TILE), lambda i, j: (i, j, 0, 0)),
        ],
        out_shape=[
            jax.ShapeDtypeStruct((b, A_HEADS, KV_RANK, t), BF16),
            jax.ShapeDtypeStruct((b, IDX_HEADS, 3 * IDX_DIM, t), BF16),
            jax.ShapeDtypeStruct((b, t, 3 * IDX_DIM), BF16),
            jax.ShapeDtypeStruct((b, IDX_HEADS, t), F32),
            jax.ShapeDtypeStruct((b, t, KV_RANK), BF16),
            jax.ShapeDtypeStruct((b, t // ATT_TILE, KV_RANK, ATT_TILE), BF16),
        ],
        compiler_params=_cparams(("arbitrary", "arbitrary")),
        name="dsa_pre",
    )(x, mod, g_mix.reshape(1, d), w_k, w_ckv.astype(BF16), w_ckv.T.astype(BF16), w_qt,
      g_cq.reshape(Q_RANK, 1), g_ckv.reshape(1, KV_RANK), g_ckv.reshape(KV_RANK, 1),
      w_uq.T.astype(BF16), jnp.swapaxes(w_uk, 1, 2).astype(BF16), w_qi.T)


def _bucket_starts():
    exact = NUM_BUCKETS // 2
    starts = list(range(1, exact + 1))
    d = exact
    for m in range(exact + 1, NUM_BUCKETS):
        while exact + int(math.log(d / exact) / math.log(MAX_DISTANCE / exact) * (NUM_BUCKETS - exact)) < m:
            d += 1
        starts.append(d)
    return tuple(starts)


BUCKET_STARTS = _bucket_starts()
FAR_DISTANCE = BUCKET_STARTS[-1]


def _bias_kernel(rb_ref, o_ref):
    hh = pl.program_id(0)
    key = lax.broadcasted_iota(I32, (ATT_TILE, ATT_TILE), 0)
    qry = lax.broadcasted_iota(I32, (ATT_TILE, ATT_TILE), 1)
    far = rb_ref[NUM_BUCKETS - 1, hh]
    for kind, offset in ((0, ATT_TILE), (1, 0)):
        dist = jnp.maximum(qry - key + offset, 0)
        acc = jnp.full((ATT_TILE, ATT_TILE), 0.0, F32) + far
        for m in range(NUM_BUCKETS - 2, -1, -1):
            acc = jnp.where(dist < BUCKET_STARTS[m], rb_ref[m, hh], acc)
        o_ref[kind, 0] = (acc - far) * LOG2E


def _bias_tiles(rel_bias):
    assert ATT_TILE >= FAR_DISTANCE
    return pl.pallas_call(
        _bias_kernel,
        grid=(A_HEADS,),
        in_specs=[pl.BlockSpec(memory_space=pltpu.SMEM)],
        out_specs=pl.BlockSpec((2, 1, ATT_TILE, ATT_TILE), lambda h: (0, h, 0, 0)),
        out_shape=jax.ShapeDtypeStruct((2, A_HEADS, ATT_TILE, ATT_TILE), F32),
        compiler_params=_cparams(("arbitrary",)),
        name="t5_bias_tiles",
    )(rel_bias)


def _dsa_attn_kernel(qi3t_ref, ki3_ref, widxt_ref, qlatt_ref, ckv_ref, ckvt_ref, bias_ref, o_ref,
                     s_scr, hi_scr, lo_scr, tie_scr, m_scr, l_scr, acc_scr, *, n_sel, idx_bits):
    qi = pl.program_id(1)
    n_chunks = qi + 1
    tq = ATT_TILE
    q_pos = qi * tq + lax.broadcasted_iota(I32, (1, tq), 1)
    k_loc = lax.broadcasted_iota(I32, (tq, 1), 0)

    w_all = widxt_ref[0]

    def score_chunk(c, carry):
        k3 = ki3_ref[0, pl.ds(pl.multiple_of(c * tq, tq), tq), :]
        sc = jnp.zeros((tq, tq), F32)
        for hh in range(IDX_HEADS):
            sc = sc + w_all[hh:hh + 1, :] * jnp.maximum(_dot(k3, qi3t_ref[0, hh]), 0.0)
        s_scr[c] = jnp.where(c * tq + k_loc <= q_pos, sc, NEG_INF)
        return carry

    lax.fori_loop(0, n_chunks, score_chunk, 0)

    @pl.when((n_chunks & 1) == 1)
    def _():
        s_scr[n_chunks] = jnp.full((tq, tq), NEG_INF, F32)

    def count(pred):
        def body(j, acc):
            c = 2 * j
            ind = (jnp.where(pred(s_scr[c], c * tq + k_loc), 1.0, 0.0).astype(F32)
                   + jnp.where(pred(s_scr[c + 1], (c + 1) * tq + k_loc), 1.0, 0.0).astype(F32))
            return acc + jnp.sum(ind.reshape(tq // 8, 8, tq), axis=0)
        acc = lax.fori_loop(0, lax.shift_right_logical(n_chunks + 1, I32(1)), body, jnp.zeros((8, tq), F32))
        return jnp.sum(acc, axis=0, keepdims=True)

    def key_to_float(key):
        bits = jnp.where(key < 0, key ^ I32(-2 ** 31), ~key)
        return lax.bitcast_convert_type(bits, F32)

    def split_chunk(c, carry):
        bits = lax.bitcast_convert_type(s_scr[c], I32)
        bits = jnp.where(bits == I32(-2 ** 31), 0, bits)
        ukey = jnp.where(bits < 0, ~bits, bits ^ I32(-2 ** 31))
        hi_scr[c] = (lax.shift_right_logical(ukey, I32(16)) - HALF_BIAS).astype(I16)
        lo_scr[c] = ((ukey & 0xFFFF) - HALF_BIAS).astype(I16)
        return carry

    lax.fori_loop(0, n_chunks, split_chunk, 0)

    @pl.when((n_chunks & 1) == 1)
    def _():
        hi_scr[n_chunks] = jnp.full((tq, tq), -HALF_BIAS, I16)
        lo_scr[n_chunks] = jnp.full((tq, tq), -HALF_BIAS, I16)

    def count16(src, pred):
        def body(j, acc):
            c = 2 * j
            ind = (jnp.where(pred(src[c]), I16(1), I16(0)) + jnp.where(pred(src[c + 1]), I16(1), I16(0)))
            rows = ind.reshape(tq // 16, 16, tq)
            part = rows[0]
            for i in range(1, tq // 16):
                part = part + rows[i]
            return acc + part
        acc = lax.fori_loop(0, lax.shift_right_logical(n_chunks + 1, I32(1)), body,
                            jnp.zeros((16, tq), I16))
        return jnp.sum(acc.astype(I32), axis=0, keepdims=True)

    def half_search(src, target):
        def bit(i, val):
            cand = val | lax.shift_left(I32(1), I32(15) - i)
            cand16 = (cand - HALF_BIAS).astype(I16)
            return jnp.where(count16(src, lambda t: t >= cand16) >= target, cand, val)
        return lax.fori_loop(0, 16, bit, jnp.zeros((1, tq), I32))

    key_hi = half_search(hi_scr, n_sel)
    hi16 = (key_hi - HALF_BIAS).astype(I16)
    n_above = count16(hi_scr, lambda t: t > hi16)

    def keep_matching(c, carry):
        lo_scr[c] = jnp.where(hi_scr[c] == hi16, lo_scr[c], I16(-HALF_BIAS))
        return carry

    lax.fori_loop(0, n_chunks, keep_matching, 0)
    key_lo = half_search(lo_scr, n_sel - n_above)
    key = lax.shift_left(key_hi, I32(16)) | key_lo
    fewer = (q_pos + 1) < n_sel
    tau = jnp.where(fewer, NEG_INF, key_to_float(key))

    n_gt = count(lambda s, kp: s > tau)
    need = jnp.where(fewer, 4.0 * ki3_ref.shape[1], n_sel - n_gt)

    def tie_bit(i, x):
        cand = x | lax.shift_left(I32(1), I32(idx_bits - 1) - i)
        cnt = count(lambda s, kp: (s == tau) & (kp < cand))
        return jnp.where(cnt < need, cand, x)

    tie_scr[...] = jnp.full(tie_scr.shape, 2 ** idx_bits - 1, I32)
    n_eq = count(lambda s, kp: s == tau)

    @pl.when(jnp.max(n_eq - need) > 0.0)
    def _():
        tie_scr[...] = lax.fori_loop(0, idx_bits, tie_bit, jnp.zeros((1, tq), I32))

    last_tie = tie_scr[...]

    def mask_chunk(c, carry):
        s = s_scr[c]
        k_pos = c * tq + k_loc
        sel = (k_pos <= q_pos) & ((s > tau) | ((s == tau) & (k_pos <= last_tie)))
        s_scr[c] = jnp.where(sel, 0.0, NEG_INF)
        return carry

    lax.fori_loop(0, n_chunks, mask_chunk, 0)

    m_scr[...] = jnp.full(m_scr.shape, NEG_INF, F32)
    l_scr[...] = jnp.zeros(l_scr.shape, F32)
    acc_scr[...] = jnp.zeros(acc_scr.shape, F32)

    def attend(c0, n_blk, bias_kinds):
        kv = ckv_ref[0, pl.ds(pl.multiple_of(c0 * tq, tq), n_blk * tq), :]
        mask = jnp.concatenate([s_scr[c0 + i] for i in range(n_blk)], axis=0)
        for hh in range(A_HEADS):
            logit = _dot(kv, qlatt_ref[0, hh]) + mask
            if bias_kinds is not None:
                logit = logit + jnp.concatenate([bias_ref[kind, hh] for kind in bias_kinds], axis=0)
            m_old = m_scr[hh:hh + 1]
            m_new = jnp.maximum(m_old, jnp.max(logit, axis=0, keepdims=True))
            m_safe = jnp.where(m_new == NEG_INF, 0.0, m_new)
            alpha = jnp.exp2(m_old - m_safe)
            p = jnp.exp2(logit - m_safe)
            l_scr[hh:hh + 1] = alpha * l_scr[hh:hh + 1] + jnp.sum(p, axis=0, keepdims=True)
            p = p.astype(BF16)
            pv = _dot(ckvt_ref[0, c0], p[:tq])
            for i in range(1, n_blk):
                pv = pv + _dot(ckvt_ref[0, c0 + i], p[i * tq:(i + 1) * tq])
            acc_scr[hh] = alpha * acc_scr[hh] + pv
            m_scr[hh:hh + 1] = m_new

    n_far = jnp.maximum(qi - 1, 0)

    def far_pair(j, carry):
        attend(2 * j, 2, None)
        return carry

    lax.fori_loop(0, lax.shift_right_logical(n_far, I32(1)), far_pair, 0)

    @pl.when((n_far & 1) == 1)
    def _():
        attend(n_far - 1, 1, None)

    @pl.when(qi >= 1)
    def _():
        attend(qi - 1, 2, (0, 1))

    @pl.when(qi == 0)
    def _():
        attend(qi, 1, (1,))

    for hh in range(A_HEADS):
        o_ref[0, hh] = (acc_scr[hh] / l_scr[hh:hh + 1]).astype(BF16)


def _dsa_attn(qi3t, ki3, widxt, qlatt, ckv, ckvt, bias):
    b, t, _ = ckv.shape
    tq = ATT_TILE
    n_sel = min(TOPK_MAX, t // 4)
    assert (t // tq) % 2 == 0
    idx_bits = max(1, (t - 1).bit_length())
    kernel = functools.partial(_dsa_attn_kernel, n_sel=n_sel, idx_bits=idx_bits)
    return pl.pallas_call(
        kernel,
        grid=(b, t // tq),
        in_specs=[
            pl.BlockSpec((1, IDX_HEADS, 3 * IDX_DIM, tq), lambda i, j: (i, 0, 0, j)),
            pl.BlockSpec((1, t, 3 * IDX_DIM), lambda i, j: (i, 0, 0)),
            pl.BlockSpec((1, IDX_HEADS, tq), lambda i, j: (i, 0, j)),
            pl.BlockSpec((1, A_HEADS, KV_RANK, tq), lambda i, j: (i, 0, 0, j)),
            pl.BlockSpec((1, t, KV_RANK), lambda i, j: (i, 0, 0)),
            pl.BlockSpec((1, t // tq, KV_RANK, tq), lambda i, j: (i, 0, 0, 0)),
            pl.BlockSpec((2, A_HEADS, tq, tq), lambda i, j: (0, 0, 0, 0)),
        ],
        out_specs=pl.BlockSpec((1, A_HEADS, KV_RANK, tq), lambda i, j: (i, 0, 0, j)),
        out_shape=jax.ShapeDtypeStruct((b, A_HEADS, KV_RANK, t), BF16),
        scratch_shapes=[
            pltpu.VMEM((t // tq, tq, tq), F32),
            pltpu.VMEM((t // tq, tq, tq), I16),
            pltpu.VMEM((t // tq, tq, tq), I16),
            pltpu.VMEM((1, tq), I32),
            pltpu.VMEM((A_HEADS, tq), F32),
            pltpu.VMEM((A_HEADS, tq), F32),
            pltpu.VMEM((A_HEADS, KV_RANK, tq), F32),
        ],
        compiler_params=_cparams(("arbitrary", "arbitrary")),
        name="dsa_attn",
    )(qi3t, ki3, widxt, qlatt, ckv, ckvt, bias)


def _dsa_post_kernel(olatt_ref, wuvt_ref, wo_ref, x_ref, mod_ref, o_ref):
    heads = [_dot(wuvt_ref[hh], olatt_ref[0, hh]).astype(BF16) for hh in range(A_HEADS)]
    ot = jnp.concatenate(heads, axis=0)
    y = lax.dot_general(ot, wo_ref[...], (((0,), (0,)), ((), ())), preferred_element_type=F32)
    o_ref[0] = x_ref[0] + mod_ref[0][2:3] * y


def _dsa_post(olatt, w_uv, w_o, x, mod):
    b, t, d = x.shape
    tm = min(ROW_TILE, t)
    return pl.pallas_call(
        _dsa_post_kernel,
        grid=(b, t // tm),
        in_specs=[
            pl.BlockSpec((1, A_HEADS, KV_RANK, tm), lambda i, j: (i, 0, 0, j)),
            pl.BlockSpec((A_HEADS, A_V_DIM, KV_RANK), lambda i, j: (0, 0, 0)),
            pl.BlockSpec((A_HEADS * A_V_DIM, d), lambda i, j: (0, 0)),
            pl.BlockSpec((1, tm, d), lambda i, j: (i, j, 0)),
            pl.BlockSpec((1, 6, d), lambda i, j: (i, 0, 0)),
        ],
        out_specs=pl.BlockSpec((1, tm, d), lambda i, j: (i, j, 0)),
        out_shape=jax.ShapeDtypeStruct((b, t, d), F32),
        compiler_params=_cparams(("arbitrary", "arbitrary")),
        name="dsa_post",
    )(olatt, jnp.swapaxes(w_uv, 1, 2).astype(BF16), w_o.astype(BF16), x, mod)


CONV_HALO = 8


def _ffn_kernel(x_ref, mod_ref, g_ref, win_ref, cw_ref, cb_ref, wout_ref, gfin_ref,
                o_ref, u_scr, carry_scr, *, final_norm):
    tm = x_ref.shape[1]
    fc = u_scr.shape[2]
    mod = mod_ref[0]
    x = x_ref[0]
    h = _norm_mod(x, g_ref[...], mod[4:5], mod[3:4]).astype(BF16)

    @pl.when(pl.program_id(1) == 0)
    def _():
        carry_scr[...] = jnp.zeros(carry_scr.shape, F32)

    y = None
    for f in range(FF_SPLIT):
        u = _dot(h, win_ref[:, f * fc:(f + 1) * fc])
        v = _dot(h, win_ref[:, D_FF + f * fc:D_FF + (f + 1) * fc])
        u_scr[f, 0:CONV_HALO] = carry_scr[f]
        u_scr[f, CONV_HALO:CONV_HALO + tm] = u
        carry_scr[f] = u[tm - CONV_HALO:tm]
        cw = cw_ref[:, f * fc:(f + 1) * fc]
        u1 = u_scr[f, CONV_HALO - 1:CONV_HALO - 1 + tm]
        u2 = u_scr[f, CONV_HALO - 2:CONV_HALO - 2 + tm]
        uc = cw[0:1] * u2 + cw[1:2] * u1 + cw[2:3] * u + cb_ref[:, f * fc:(f + 1) * fc]
        act = 0.5 * uc * (1.0 + lax.erf(uc * (2.0 ** -0.5)))
        part = _dot((act * v).astype(BF16), wout_ref[f * fc:(f + 1) * fc, :])
        y = part if y is None else y + part
    out = x + mod[5:6] * y
    if final_norm:
        out = _rms(out) * gfin_ref[...]
    o_ref[0] = out


def _conv_ffn(x, mod, g_ffn, w_in, conv_w, conv_b, w_out, g_final, final_norm):
    b, t, d = x.shape
    tm = min(ROW_TILE, t)
    fc = D_FF // FF_SPLIT
    assert fc % LANES == 0
    kernel = functools.partial(_ffn_kernel, final_norm=final_norm)
    resident = lambda *shape: pl.BlockSpec(shape, lambda i, j: (0,) * len(shape),
                                           pipeline_mode=pl.Buffered(1))
    return pl.pallas_call(
        kernel,
        grid=(b, t // tm),
        in_specs=[
            pl.BlockSpec((1, tm, d), lambda i, j: (i, j, 0)),
            pl.BlockSpec((1, 6, d), lambda i, j: (i, 0, 0)),
            resident(1, d),
            resident(d, 2 * D_FF),
            resident(3, D_FF),
            resident(1, D_FF),
            resident(D_FF, d),
            resident(1, d),
        ],
        out_specs=pl.BlockSpec((1, tm, d), lambda i, j: (i, j, 0)),
        out_shape=jax.ShapeDtypeStruct((b, t, d), F32),
        scratch_shapes=[
            pltpu.VMEM((FF_SPLIT, tm + CONV_HALO, fc), F32),
            pltpu.VMEM((FF_SPLIT, CONV_HALO, fc), F32),
        ],
        compiler_params=_cparams(("arbitrary", "arbitrary")),
        name="conv_ffn_final" if final_norm else "conv_ffn",
    )(x, mod, g_ffn.reshape(1, d), w_in.astype(BF16), conv_w, conv_b.reshape(1, D_FF),
      w_out.astype(BF16), g_final.reshape(1, d))


def _gla_pre_kernel(x_ref, mod_ref, g_ref, wqkr_ref, wv_ref, wg_ref, wg2_ref, bg_ref,
                    q_ref, k_ref, r_ref, v_ref, la_ref):
    mod = mod_ref[0]
    h = _norm_mod(x_ref[0], g_ref[...], mod[1:2], mod[0:1])
    hb = h.astype(BF16)
    qkr = _dot(hb, wqkr_ref[...])
    q_ref[0] = (qkr[:, :G_KD] * (G_DK ** -0.5)).astype(BF16)
    k_ref[0] = qkr[:, G_KD:2 * G_KD].astype(BF16)
    r_ref[0] = qkr[:, 2 * G_KD:].astype(BF16)
    v_ref[0] = _dot(hb, wv_ref[...]).astype(BF16)
    g_lr = _dot_hp(h, wg_ref[...])
    gate = _dot_hp(g_lr, wg2_ref[...]) + bg_ref[...]
    log_sig = jnp.minimum(gate, 0.0) - jnp.log1p(jnp.exp(-jnp.abs(gate)))
    la_ref[0] = log_sig / GATE_TAU


def _gla_pre(x, mod, g_mix, w_in, w_g2, b_g):
    b, t, d = x.shape
    tm = min(ROW_TILE, t)
    w_q, w_k, w_v, w_r, w_g = jnp.split(w_in, [G_KD, 2 * G_KD, 2 * G_KD + G_VD, 2 * G_KD + 2 * G_VD], axis=1)
    w_qkr = jnp.concatenate([w_q, w_k, w_r], axis=1).astype(BF16)
    w_g_pad = jnp.concatenate([w_g, jnp.zeros((d, LANES - G_RANK), F32)], axis=1)
    w_g2_pad = jnp.concatenate([w_g2, jnp.zeros((LANES - G_RANK, G_KD), F32)], axis=0)
    const = lambda *shape: pl.BlockSpec(shape, lambda i, j: (0,) * len(shape))
    row = lambda width: pl.BlockSpec((1, tm, width), lambda i, j: (i, j, 0))
    return pl.pallas_call(
        _gla_pre_kernel,
        grid=(b, t // tm),
        in_specs=[
            row(d),
            pl.BlockSpec((1, 6, d), lambda i, j: (i, 0, 0)),
            const(1, d),
            const(d, 2 * G_KD + G_VD),
            const(d, G_VD),
            const(d, LANES),
            const(LANES, G_KD),
            const(1, G_KD),
        ],
        out_specs=[row(G_KD), row(G_KD), row(G_VD), row(G_VD), row(G_KD)],
        out_shape=[
            jax.ShapeDtypeStruct((b, t, G_KD), BF16),
            jax.ShapeDtypeStruct((b, t, G_KD), BF16),
            jax.ShapeDtypeStruct((b, t, G_VD), BF16),
            jax.ShapeDtypeStruct((b, t, G_VD), BF16),
            jax.ShapeDtypeStruct((b, t, G_KD), F32),
        ],
        compiler_params=_cparams(("arbitrary", "arbitrary")),
        name="gla_pre",
    )(x, mod, g_mix.reshape(1, d), w_qkr, w_v.astype(BF16), w_g_pad, w_g2_pad, b_g.reshape(1, G_KD))


GLA_LEVELS = tuple(1 << p for p in range(GLA_CHUNK.bit_length() - 1))


def _gla_kernel(q_ref, k_ref, g_ref, v_ref, o_ref, st_scr):
    n_ch = q_ref.shape[1] // GLA_CHUNK
    cl = GLA_CHUNK

    @pl.when(pl.program_id(2) == 0)
    def _():
        st_scr[...] = jnp.zeros(st_scr.shape, F32)

    def lanes(ref):
        x = ref[0]
        return jnp.concatenate([x[c * cl:(c + 1) * cl] for c in range(n_ch)], axis=1)

    q = lanes(q_ref).astype(F32)
    k = lanes(k_ref).astype(F32)
    g_hi, g_lo = _split2(lanes(g_ref))
    g_pieces = jnp.concatenate([g_hi, g_lo], axis=0)

    r = lax.broadcasted_iota(I32, ((len(GLA_LEVELS) + 1) * cl, 2 * cl), 0)
    cidx = lax.broadcasted_iota(I32, ((len(GLA_LEVELS) + 1) * cl, 2 * cl), 1) & (cl - 1)
    i_loc = r & (cl - 1)
    lvl = lax.shift_right_logical(r, I32(cl.bit_length() - 1))
    half = lax.shift_left(I32(1), lvl)
    ref_row = jnp.where(lvl < len(GLA_LEVELS), (i_loc & ~(2 * half - 1)) | (half - 1), -1)
    sel = (jnp.where(cidx <= i_loc, 1.0, 0.0) - jnp.where(cidx <= ref_row, 1.0, 0.0)).astype(BF16)
    d_all = _dot(sel, g_pieces)

    row = lax.broadcasted_iota(I32, (cl, 1), 0)
    xor = lax.broadcasted_iota(I32, (cl, cl), 0) ^ lax.broadcasted_iota(I32, (cl, cl), 1)

    a = [jnp.where(xor == 0, _dot_nt(q[:, c * G_DK:(c + 1) * G_DK].astype(BF16),
                                     k[:, c * G_DK:(c + 1) * G_DK].astype(BF16)), 0.0)
         for c in range(n_ch)]
    for li, s in enumerate(GLA_LEVELS):
        dl = d_all[li * cl:(li + 1) * cl]
        hi = (row & s) != 0
        e = jnp.exp(jnp.where(hi, dl, -dl))
        qt = jnp.where(hi, q * e, 0.0).astype(BF16)
        kt = jnp.where(hi, 0.0, k * e).astype(BF16)
        for c in range(n_ch):
            p = _dot_nt(qt[:, c * G_DK:(c + 1) * G_DK], kt[:, c * G_DK:(c + 1) * G_DK])
            a[c] = a[c] + jnp.where(xor < 2 * s, p, 0.0)

    b = d_all[len(GLA_LEVELS) * cl:]
    b_last = b[cl - 1:cl]
    q_hat = (q * jnp.exp(b)).astype(BF16)
    k_hat = (k * jnp.exp(b_last - b)).astype(BF16)
    decay = jnp.exp(b_last)

    st = st_scr[...]
    v_all = v_ref[0]
    outs = []
    for c in range(n_ch):
        sl = slice(c * G_DK, (c + 1) * G_DK)
        v_c = v_all[c * cl:(c + 1) * cl]
        o_inter = _dot_nt(q_hat[:, sl], st.astype(BF16))
        o_intra = _dot(a[c].astype(BF16), v_c)
        outs.append(o_inter + o_intra)
        st = st * decay[:, sl] + lax.dot_general(v_c, k_hat[:, sl], (((0,), (0,)), ((), ())),
                                                 preferred_element_type=F32)
    st_scr[...] = st
    o_ref[0] = jnp.concatenate(outs, axis=0).astype(BF16)


def _gla_core(q, k, la, v):
    b, t, _ = q.shape
    tm = min(GLA_TILE, t)
    assert G_DK == LANES and tm % GLA_CHUNK == 0
    return pl.pallas_call(
        _gla_kernel,
        grid=(b, G_HEADS, t // tm),
        in_specs=[
            pl.BlockSpec((1, tm, G_DK), lambda i, h, j: (i, j, h)),
            pl.BlockSpec((1, tm, G_DK), lambda i, h, j: (i, j, h)),
            pl.BlockSpec((1, tm, G_DK), lambda i, h, j: (i, j, h)),
            pl.BlockSpec((1, tm, G_DV), lambda i, h, j: (i, j, h)),
        ],
        out_specs=pl.BlockSpec((1, tm, G_DV), lambda i, h, j: (i, j, h)),
        out_shape=jax.ShapeDtypeStruct((b, t, G_VD), BF16),
        scratch_shapes=[pltpu.VMEM((G_DV, G_DK), F32)],
        compiler_params=_cparams(("arbitrary", "arbitrary", "arbitrary")),
        name="gla_core",
    )(q, k, la, v)


def _gla_post_kernel(o_ref, r_ref, gn_ref, wo_ref, x_ref, mod_ref, out_ref):
    o = o_ref[0].astype(F32)
    gn = gn_ref[...]
    heads = [_rms(o[:, hh * G_DV:(hh + 1) * G_DV]) * gn[:, hh * G_DV:(hh + 1) * G_DV]
             for hh in range(G_HEADS)]
    r = r_ref[0].astype(F32)
    z = jnp.concatenate(heads, axis=1) * (r * jax.nn.sigmoid(r))
    out_ref[0] = x_ref[0] + mod_ref[0][2:3] * _dot(z.astype(BF16), wo_ref[...])


def _gla_post(o, r, g_norm, w_o, x, mod):
    b, t, d = x.shape
    tm = min(ROW_TILE, t)
    row = lambda width: pl.BlockSpec((1, tm, width), lambda i, j: (i, j, 0))
    return pl.pallas_call(
        _gla_post_kernel,
        grid=(b, t // tm),
        in_specs=[row(G_VD), row(G_VD),
                  pl.BlockSpec((1, G_VD), lambda i, j: (0, 0)),
                  pl.BlockSpec((G_VD, d), lambda i, j: (0, 0)),
                  row(d),
                  pl.BlockSpec((1, 6, d), lambda i, j: (i, 0, 0))],
        out_specs=row(d),
        out_shape=jax.ShapeDtypeStruct((b, t, d), F32),
        compiler_params=_cparams(("arbitrary", "arbitrary")),
        name="gla_post",
    )(o, r, g_norm.reshape(1, G_VD), w_o.astype(BF16), x, mod)


def kernel(x, c, rel_bias, a_w_in, a_g_cq, a_g_ckv, a_w_uq, a_w_uk, a_w_uv, a_w_qi, a_w_o, b_w_in, b_w_g2, b_b_g, b_g_norm, b_w_o, ada_w, ada_b, g_mix, g_ffn, f_w_in, f_conv_w, f_conv_b, f_w_out, g_final):
    b, t, d = x.shape
    depth = ada_w.shape[0]
    assert d == D_MODEL and depth == 2 and t % ROW_TILE == 0
    mod = _modulation(c, ada_w, ada_b).reshape(depth, b, 6, d)

    qlatt, qi3t, ki3, widxt, ckv, ckvt = _dsa_pre(x, mod[0], g_mix[0], a_w_in[0], a_g_cq[0], a_g_ckv[0],
                                                  a_w_uq[0], a_w_uk[0], a_w_qi[0])
    olatt = _dsa_attn(qi3t, ki3, widxt, qlatt, ckv, ckvt, _bias_tiles(rel_bias))
    x = _dsa_post(olatt, a_w_uv[0], a_w_o[0], x, mod[0])
    x = _conv_ffn(x, mod[0], g_ffn[0], f_w_in[0], f_conv_w[0], f_conv_b[0], f_w_out[0], g_final, False)

    q, k, r, v, la = _gla_pre(x, mod[1], g_mix[1], b_w_in[0], b_w_g2[0], b_b_g[0])
    o = _gla_core(q, k, la, v)
    x = _gla_post(o, r, b_g_norm[0], b_w_o[0], x, mod[1])
    return _conv_ffn(x, mod[1], g_ffn[1], f_w_in[1], f_conv_w[1], f_conv_b[1], f_w_out[1], g_final, True)
```

```python
import functools
import math

import numpy as np
import jax
import jax.numpy as jnp
from jax import lax
from jax.experimental import pallas as pl
from jax.experimental.pallas import tpu as pltpu

F32 = jnp.float32
BF16 = jnp.bfloat16
I32 = jnp.int32
I16 = jnp.int16

D_MODEL = 1024
A_HEADS = 8
A_HEAD_DIM = 128
A_V_DIM = 128
Q_RANK = 256
KV_RANK = 256
IDX_HEADS = 8
IDX_DIM = 64
TOPK_MAX = 256
NUM_BUCKETS = 32
MAX_DISTANCE = 128
G_HEADS = 4
G_KD = 512
G_VD = 1024
G_DK = G_KD // G_HEADS
G_DV = G_VD // G_HEADS
G_RANK = 16
GATE_TAU = 16.0
D_FF = 2816
EPS = 1e-6

LANES = 128
VMEM_LIMIT_BYTES = 56 * 1024 * 1024

ROW_TILE = 512
ATT_TILE = 256
GLA_TILE = 512
GLA_CHUNK = 128
FF_SPLIT = 2
NEG_INF = float("-inf")
HALF_BIAS = 1 << 15
LOG2E = math.log2(math.e)


def _cparams(sem):
    return pltpu.CompilerParams(dimension_semantics=sem, vmem_limit_bytes=VMEM_LIMIT_BYTES)


def _dot(a, b):
    return lax.dot_general(a, b, (((1,), (0,)), ((), ())), preferred_element_type=F32)


def _dot_nt(a, b):
    return lax.dot_general(a, b, (((1,), (1,)), ((), ())), preferred_element_type=F32)


def _split2(a):
    hi = a.astype(BF16)
    lo = (a - hi.astype(F32)).astype(BF16)
    return hi, lo


def _dot_hp(a, b):
    ah, al = _split2(a)
    bh, bl = _split2(b)
    return _dot(ah, bh) + (_dot(ah, bl) + _dot(al, bh))


def _dot_hp_nt(a, b):
    ah, al = _split2(a)
    bh, bl = _split2(b)
    return _dot_nt(ah, bh) + (_dot_nt(ah, bl) + _dot_nt(al, bh))


def _hi_lo_concat(a, order):
    hi = a.astype(BF16).astype(F32)
    parts = (hi, a - hi)
    return jnp.concatenate([parts[o] for o in order], axis=1).astype(BF16)


def _rms(x):
    return x * lax.rsqrt(jnp.mean(x * x, axis=-1, keepdims=True) + EPS)


def _norm_mod(x, g, sc, sh):
    return (_rms(x) * g) * (1.0 + sc) + sh


def _mod_kernel(c_ref, w_ref, b_ref, o_ref):
    c = c_ref[...]
    cond = c * jax.nn.sigmoid(c)
    o_ref[0] = _dot_hp(cond, w_ref[0]) + b_ref[0]


def _modulation(c, ada_w, ada_b):
    depth, d, six_d = ada_w.shape
    b = c.shape[0]
    n_col = six_d // d
    return pl.pallas_call(
        _mod_kernel,
        grid=(depth, n_col),
        in_specs=[
            pl.BlockSpec((b, d), lambda l, j: (0, 0)),
            pl.BlockSpec((1, d, d), lambda l, j: (l, 0, j)),
            pl.BlockSpec((1, 1, d), lambda l, j: (l, 0, j)),
        ],
        out_specs=pl.BlockSpec((1, b, d), lambda l, j: (l, 0, j)),
        out_shape=jax.ShapeDtypeStruct((depth, b, six_d), F32),
        compiler_params=_cparams(("arbitrary", "arbitrary")),
        name="adaln_mod",
    )(c, ada_w, ada_b.reshape(depth, 1, six_d))


def _rms_cols(xt):
    return xt * lax.rsqrt(jnp.mean(xt * xt, axis=0, keepdims=True) + EPS)


def _dsa_pre_kernel(x_ref, mod_ref, g_ref, wk_ref, wkv_ref, wkvt_ref, wqt_ref, gcq_ref, gckv_ref,
                    gckvc_ref, wuqt_ref, wukt_ref, wqit_ref,
                    qlatt_ref, qi3t_ref, ki3_ref, widxt_ref, ckv_ref, ckvt_ref):
    mod = mod_ref[0]
    h = _norm_mod(x_ref[0], g_ref[...], mod[1:2], mod[0:1])
    hb = h.astype(BF16)

    k_idx = _dot_hp(h, wk_ref[...])[:, :IDX_DIM]
    ki3_ref[0] = _hi_lo_concat(k_idx, (0, 1, 0))
    ckv_ref[0] = (_rms(_dot(hb, wkv_ref[...])) * gckv_ref[...]).astype(BF16)
    c_kvt = (_rms_cols(_dot_nt(wkvt_ref[...], hb)) * gckvc_ref[...]).astype(BF16)
    for j in range(ckvt_ref.shape[1]):
        ckvt_ref[0, j] = c_kvt[:, j * ATT_TILE:(j + 1) * ATT_TILE]

    hpt = _dot_hp_nt(wqt_ref[...], h)
    widxt_ref[0] = hpt[Q_RANK:] * (IDX_HEADS ** -0.5 * IDX_DIM ** -0.5)
    c_qt = _rms_cols(hpt[:Q_RANK]) * gcq_ref[...]
    qt = _dot(wuqt_ref[...], c_qt.astype(BF16))
    for hh in range(A_HEADS):
        qh = qt[hh * A_HEAD_DIM:(hh + 1) * A_HEAD_DIM].astype(BF16)
        qlatt_ref[0, hh] = (_dot(wukt_ref[hh], qh) * (A_HEAD_DIM ** -0.5 * LOG2E)).astype(BF16)
    q_idxt = _dot_hp(wqit_ref[...], c_qt)
    for hh in range(IDX_HEADS):
        piece = q_idxt[hh * IDX_DIM:(hh + 1) * IDX_DIM]
        hi = piece.astype(BF16).astype(F32)
        qi3t_ref[0, hh] = jnp.concatenate([hi, hi, piece - hi], axis=0).astype(BF16)


def _dsa_pre(x, mod, g_mix, w_in, g_cq, g_ckv, w_uq, w_uk, w_qi):
    b, t, d = x.shape
    tm = min(ROW_TILE, t)
    w_cq, w_ckv, w_ki, w_wi = jnp.split(w_in, [Q_RANK, Q_RANK + KV_RANK, Q_RANK + KV_RANK + IDX_DIM], axis=1)
    w_k = jnp.concatenate([w_ki, jnp.zeros((d, LANES - IDX_DIM), F32)], axis=1)
    w_qt = jnp.concatenate([w_cq, w_wi], axis=1).T
    const = lambda *shape: pl.BlockSpec(shape, lambda i, j: (0,) * len(shape))
    return pl.pallas_call(
        _dsa_pre_kernel,
        grid=(b, t // tm),
        in_specs=[
            pl.BlockSpec((1, tm, d), lambda i, j: (i, j, 0)),
            pl.BlockSpec((1, 6, d), lambda i, j: (i, 0, 0)),
            const(1, d),
            const(d, LANES),
            const(d, KV_RANK),
            const(KV_RANK, d),
            const(Q_RANK + IDX_HEADS, d),
            const(Q_RANK, 1),
            const(1, KV_RANK),
            const(KV_RANK, 1),
            const(A_HEADS * A_HEAD_DIM, Q_RANK),
            const(A_HEADS, KV_RANK, A_HEAD_DIM),
            const(IDX_HEADS * IDX_DIM, Q_RANK),
        ],
        out_specs=[
            pl.BlockSpec((1, A_HEADS, KV_RANK, tm), lambda i, j: (i, 0, 0, j)),
            pl.BlockSpec((1, IDX_HEADS, 3 * IDX_DIM, tm), lambda i, j: (i, 0, 0, j)),
            pl.BlockSpec((1, tm, 3 * IDX_DIM), lambda i, j: (i, j, 0)),
            pl.BlockSpec((1, IDX_HEADS, tm), lambda i, j: (i, 0, j)),
            pl.BlockSpec((1, tm, KV_RANK), lambda i, j: (i, j, 0)),
            pl.BlockSpec((1, tm // ATT_TILE, KV_RANK, ATT_TILE), lambda i, j: (i, j, 0, 0)),
        ],
        out_shape=[
            jax.ShapeDtypeStruct((b, A_HEADS, KV_RANK, t), BF16),
            jax.ShapeDtypeStruct((b, IDX_HEADS, 3 * IDX_DIM, t), BF16),
            jax.ShapeDtypeStruct((b, t, 3 * IDX_DIM), BF16),
            jax.ShapeDtypeStruct((b, IDX_HEADS, t), F32),
            jax.ShapeDtypeStruct((b, t, KV_RANK), BF16),
            jax.ShapeDtypeStruct((b, t // ATT_TILE, KV_RANK, ATT_TILE), BF16),
        ],
        compiler_params=_cparams(("arbitrary", "arbitrary")),
        name="dsa_pre",
    )(x, mod, g_mix.reshape(1, d), w_k, w_ckv.astype(BF16), w_ckv.T.astype(BF16), w_qt,
      g_cq.reshape(Q_RANK, 1), g_ckv.reshape(1, KV_RANK), g_ckv.reshape(KV_RANK, 1),
      w_uq.T.astype(BF16), jnp.swapaxes(w_uk, 1, 2).astype(BF16), w_qi.T)


def _bucket_starts():
    exact = NUM_BUCKETS // 2
    starts = list(range(1, exact + 1))
    d = exact
    for m in range(exact + 1, NUM_BUCKETS):
        while exact + int(math.log(d / exact) / math.log(MAX_DISTANCE / exact) * (NUM_BUCKETS - exact)) < m:
            d += 1
        starts.append(d)
    return tuple(starts)


BUCKET_STARTS = _bucket_starts()
FAR_DISTANCE = BUCKET_STARTS[-1]


def _bias_kernel(rb_ref, o_ref):
    hh = pl.program_id(0)
    key = lax.broadcasted_iota(I32, (ATT_TILE, ATT_TILE), 0)
    qry = lax.broadcasted_iota(I32, (ATT_TILE, ATT_TILE), 1)
    far = rb_ref[NUM_BUCKETS - 1, hh]
    for kind, offset in ((0, ATT_TILE), (1, 0)):
        dist = jnp.maximum(qry - key + offset, 0)
        acc = jnp.full((ATT_TILE, ATT_TILE), 0.0, F32) + far
        for m in range(NUM_BUCKETS - 2, -1, -1):
            acc = jnp.where(dist < BUCKET_STARTS[m], rb_ref[m, hh], acc)
        o_ref[kind, 0] = (acc - far) * LOG2E


def _bias_tiles(rel_bias):
    assert ATT_TILE >= FAR_DISTANCE
    return pl.pallas_call(
        _bias_kernel,
        grid=(A_HEADS,),
        in_specs=[pl.BlockSpec(memory_space=pltpu.SMEM)],
        out_specs=pl.BlockSpec((2, 1, ATT_TILE, ATT_TILE), lambda h: (0, h, 0, 0)),
        out_shape=jax.ShapeDtypeStruct((2, A_HEADS, ATT_TILE, ATT_TILE), F32),
        compiler_params=_cparams(("arbitrary",)),
        name="t5_bias_tiles",
    )(rel_bias)


def _dsa_attn_kernel(qi3t_ref, ki3_ref, widxt_ref, qlatt_ref, ckv_ref, ckvt_ref, bias_ref, o_ref,
                     s_scr, hi_scr, lo_scr, tie_scr, m_scr, l_scr, acc_scr, *, n_sel, idx_bits):
    qi = pl.program_id(1)
    n_chunks = qi + 1
    tq = ATT_TILE
    q_pos = qi * tq + lax.broadcasted_iota(I32, (1, tq), 1)
    k_loc = lax.broadcasted_iota(I32, (tq, 1), 0)

    w_all = widxt_ref[0]

    def score_chunk(c, carry):
        k3 = ki3_ref[0, pl.ds(pl.multiple_of(c * tq, tq), tq), :]
        sc = jnp.zeros((tq, tq), F32)
        for hh in range(IDX_HEADS):
            sc = sc + w_all[hh:hh + 1, :] * jnp.maximum(_dot(k3, qi3t_ref[0, hh]), 0.0)
        s_scr[c] = jnp.where(c * tq + k_loc <= q_pos, sc, NEG_INF)
        return carry

    lax.fori_loop(0, n_chunks, score_chunk, 0)

    @pl.when((n_chunks & 1) == 1)
    def _():
        s_scr[n_chunks] = jnp.full((tq, tq), NEG_INF, F32)

    def count(pred):
        def body(j, acc):
            c = 2 * j
            ind = (jnp.where(pred(s_scr[c], c * tq + k_loc), 1.0, 0.0).astype(F32)
                   + jnp.where(pred(s_scr[c + 1], (c + 1) * tq + k_loc), 1.0, 0.0).astype(F32))
            return acc + jnp.sum(ind.reshape(tq // 8, 8, tq), axis=0)
        acc = lax.fori_loop(0, lax.shift_right_logical(n_chunks + 1, I32(1)), body, jnp.zeros((8, tq), F32))
        return jnp.sum(acc, axis=0, keepdims=True)

    def key_to_float(key):
        bits = jnp.where(key < 0, key ^ I32(-2 ** 31), ~key)
        return lax.bitcast_convert_type(bits, F32)

    def split_chunk(c, carry):
        bits = lax.bitcast_convert_type(s_scr[c], I32)
        bits = jnp.where(bits == I32(-2 ** 31), 0, bits)
        ukey = jnp.where(bits < 0, ~bits, bits ^ I32(-2 ** 31))
        hi_scr[c] = (lax.shift_right_logical(ukey, I32(16)) - HALF_BIAS).astype(I16)
        lo_scr[c] = ((ukey & 0xFFFF) - HALF_BIAS).astype(I16)
        return carry

    lax.fori_loop(0, n_chunks, split_chunk, 0)

    @pl.when((n_chunks & 1) == 1)
    def _():
        hi_scr[n_chunks] = jnp.full((tq, tq), -HALF_BIAS, I16)
        lo_scr[n_chunks] = jnp.full((tq, tq), -HALF_BIAS, I16)

    def count16(src, pred):
        def body(j, acc):
            c = 2 * j
            ind = (jnp.where(pred(src[c]), I16(1), I16(0)) + jnp.where(pred(src[c + 1]), I16(1), I16(0)))
            rows = ind.reshape(tq // 16, 16, tq)
            part = rows[0]
            for i in range(1, tq // 16):
                part = part + rows[i]
            return acc + part
        acc = lax.fori_loop(0, lax.shift_right_logical(n_chunks + 1, I32(1)), body,
                            jnp.zeros((16, tq), I16))
        return jnp.sum(acc.astype(I32), axis=0, keepdims=True)

    def half_search(src, target):
        def bit(i, val):
            cand = val | lax.shift_left(I32(1), I32(15) - i)
            cand16 = (cand - HALF_BIAS).astype(I16)
            return jnp.where(count16(src, lambda t: t >= cand16) >= target, cand, val)
        return lax.fori_loop(0, 16, bit, jnp.zeros((1, tq), I32))

    key_hi = half_search(hi_scr, n_sel)
    hi16 = (key_hi - HALF_BIAS).astype(I16)
    n_above = count16(hi_scr, lambda t: t > hi16)

    def keep_matching(c, carry):
        lo_scr[c] = jnp.where(hi_scr[c] == hi16, lo_scr[c], I16(-HALF_BIAS))
        return carry

    lax.fori_loop(0, n_chunks, keep_matching, 0)
    key_lo = half_search(lo_scr, n_sel - n_above)
    key = lax.shift_left(key_hi, I32(16)) | key_lo
    fewer = (q_pos + 1) < n_sel
    tau = jnp.where(fewer, NEG_INF, key_to_float(key))

    n_gt = count(lambda s, kp: s > tau)
    need = jnp.where(fewer, 4.0 * ki3_ref.shape[1], n_sel - n_gt)

    def tie_bit(i, x):
        cand = x | lax.shift_left(I32(1), I32(idx_bits - 1) - i)
        cnt = count(lambda s, kp: (s == tau) & (kp < cand))
        return jnp.where(cnt < need, cand, x)

    tie_scr[...] = jnp.full(tie_scr.shape, 2 ** idx_bits - 1, I32)
    n_eq = count(lambda s, kp: s == tau)

    @pl.when(jnp.max(n_eq - need) > 0.0)
    def _():
        tie_scr[...] = lax.fori_loop(0, idx_bits, tie_bit, jnp.zeros((1, tq), I32))

    last_tie = tie_scr[...]

    def mask_chunk(c, carry):
        s = s_scr[c]
        k_pos = c * tq + k_loc
        sel = (k_pos <= q_pos) & ((s > tau) | ((s == tau) & (k_pos <= last_tie)))
        s_scr[c] = jnp.where(sel, 0.0, NEG_INF)
        return carry

    lax.fori_loop(0, n_chunks, mask_chunk, 0)

    m_scr[...] = jnp.full(m_scr.shape, NEG_INF, F32)
    l_scr[...] = jnp.zeros(l_scr.shape, F32)
    acc_scr[...] = jnp.zeros(acc_scr.shape, F32)

    def attend(c0, n_blk, bias_kinds):
        kv = ckv_ref[0, pl.ds(pl.multiple_of(c0 * tq, tq), n_blk * tq), :]
        mask = jnp.concatenate([s_scr[c0 + i] for i in range(n_blk)], axis=0)
        for hh in range(A_HEADS):
            logit = _dot(kv, qlatt_ref[0, hh]) + mask
            if bias_kinds is not None:
                logit = logit + jnp.concatenate([bias_ref[kind, hh] for kind in bias_kinds], axis=0)
            m_old = m_scr[hh:hh + 1]
            m_new = jnp.maximum(m_old, jnp.max(logit, axis=0, keepdims=True))
            m_safe = jnp.where(m_new == NEG_INF, 0.0, m_new)
            alpha = jnp.exp2(m_old - m_safe)
            p = jnp.exp2(logit - m_safe)
            l_scr[hh:hh + 1] = alpha * l_scr[hh:hh + 1] + jnp.sum(p, axis=0, keepdims=True)
            p = p.astype(BF16)
            pv = _dot(ckvt_ref[0, c0], p[:tq])
            for i in range(1, n_blk):
                pv = pv + _dot(ckvt_ref[0, c0 + i], p[i * tq:(i + 1) * tq])
            acc_scr[hh] = alpha * acc_scr[hh] + pv
            m_scr[hh:hh + 1] = m_new

    n_far = jnp.maximum(qi - 1, 0)

    def far_pair(j, carry):
        attend(2 * j, 2, None)
        return carry

    lax.fori_loop(0, lax.shift_right_logical(n_far, I32(1)), far_pair, 0)

    @pl.when((n_far & 1) == 1)
    def _():
        attend(n_far - 1, 1, None)

    @pl.when(qi >= 1)
    def _():
        attend(qi - 1, 2, (0, 1))

    @pl.when(qi == 0)
    def _():
        attend(qi, 1, (1,))

    for hh in range(A_HEADS):
        o_ref[0, hh] = (acc_scr[hh] / l_scr[hh:hh + 1]).astype(BF16)


def _dsa_attn(qi3t, ki3, widxt, qlatt, ckv, ckvt, bias):
    b, t, _ = ckv.shape
    tq = ATT_TILE
    n_sel = min(TOPK_MAX, t // 4)
    assert (t // tq) % 2 == 0
    idx_bits = max(1, (t - 1).bit_length())
    kernel = functools.partial(_dsa_attn_kernel, n_sel=n_sel, idx_bits=idx_bits)
    return pl.pallas_call(
        kernel,
        grid=(b, t // tq),
        in_specs=[
            pl.BlockSpec((1, IDX_HEADS, 3 * IDX_DIM, tq), lambda i, j: (i, 0, 0, j)),
            pl.BlockSpec((1, t, 3 * IDX_DIM), lambda i, j: (i, 0, 0)),
            pl.BlockSpec((1, IDX_HEADS, tq), lambda i, j: (i, 0, j)),
            pl.BlockSpec((1, A_HEADS, KV_RANK, tq), lambda i, j: (i, 0, 0, j)),
            pl.BlockSpec((1, t, KV_RANK), lambda i, j: (i, 0, 0)),
            pl.BlockSpec((1, t // tq, KV_RANK, tq), lambda i, j: (i, 0, 0, 0)),
            pl.BlockSpec((2, A_HEADS, tq, tq), lambda i, j: (0, 0, 0, 0)),
        ],
        out_specs=pl.BlockSpec((1, A_HEADS, KV_RANK, tq), lambda i, j: (i, 0, 0, j)),
        out_shape=jax.ShapeDtypeStruct((b, A_HEADS, KV_RANK, t), BF16),
        scratch_shapes=[
            pltpu.VMEM((t // tq, tq, tq), F32),
            pltpu.VMEM((t // tq, tq, tq), I16),
            pltpu.VMEM((t // tq, tq, tq), I16),
            pltpu.VMEM((1, tq), I32),
            pltpu.VMEM((A_HEADS, tq), F32),
            pltpu.VMEM((A_HEADS, tq), F32),
            pltpu.VMEM((A_HEADS, KV_RANK, tq), F32),
        ],
        compiler_params=_cparams(("arbitrary", "arbitrary")),
        name="dsa_attn",
    )(qi3t, ki3, widxt, qlatt, ckv, ckvt, bias)


CONV_HALO = 8


def _dsa_out(olatt_ref, wuvt_ref, wo_ref):
    heads = [_dot(wuvt_ref[hh], olatt_ref[0, hh]).astype(BF16) for hh in range(A_HEADS)]
    ot = jnp.concatenate(heads, axis=0)
    return lax.dot_general(ot, wo_ref[...], (((0,), (0,)), ((), ())), preferred_element_type=F32)


def _gla_out(o_ref, r_ref, gn_ref, wo_ref):
    o = o_ref[0].astype(F32)
    gn = gn_ref[...]
    heads = [_rms(o[:, hh * G_DV:(hh + 1) * G_DV]) * gn[:, hh * G_DV:(hh + 1) * G_DV]
             for hh in range(G_HEADS)]
    r = r_ref[0].astype(F32)
    z = jnp.concatenate(heads, axis=1) * (r * jax.nn.sigmoid(r))
    return _dot(z.astype(BF16), wo_ref[...])


def _ffn_kernel(*refs, mixer_out, n_mixer, final_norm):
    mixer_refs = refs[:n_mixer]
    (x_ref, mod_ref, g_ref, win_ref, cw_ref, cb_ref, wout_ref, gfin_ref,
     o_ref, u_scr, carry_scr) = refs[n_mixer:]
    tm = x_ref.shape[1]
    fc = u_scr.shape[2]
    mod = mod_ref[0]
    x = x_ref[0] + mod[2:3] * mixer_out(*mixer_refs)
    h = _norm_mod(x, g_ref[...], mod[4:5], mod[3:4]).astype(BF16)

    @pl.when(pl.program_id(1) == 0)
    def _():
        carry_scr[...] = jnp.zeros(carry_scr.shape, F32)

    y = None
    for f in range(FF_SPLIT):
        u = _dot(h, win_ref[:, f * fc:(f + 1) * fc])
        v = _dot(h, win_ref[:, D_FF + f * fc:D_FF + (f + 1) * fc])
        u_scr[f, 0:CONV_HALO] = carry_scr[f]
        u_scr[f, CONV_HALO:CONV_HALO + tm] = u
        carry_scr[f] = u[tm - CONV_HALO:tm]
        cw = cw_ref[:, f * fc:(f + 1) * fc]
        u1 = u_scr[f, CONV_HALO - 1:CONV_HALO - 1 + tm]
        u2 = u_scr[f, CONV_HALO - 2:CONV_HALO - 2 + tm]
        uc = cw[0:1] * u2 + cw[1:2] * u1 + cw[2:3] * u + cb_ref[:, f * fc:(f + 1) * fc]
        act = 0.5 * uc * (1.0 + lax.erf(uc * (2.0 ** -0.5)))
        part = _dot((act * v).astype(BF16), wout_ref[f * fc:(f + 1) * fc, :])
        y = part if y is None else y + part
    out = x + mod[5:6] * y
    if final_norm:
        out = _rms(out) * gfin_ref[...]
    o_ref[0] = out


def _resident(*shape):
    return pl.BlockSpec(shape, lambda i, j: (0,) * len(shape), pipeline_mode=pl.Buffered(1))


def _mixer_ffn(mixer_out, mixer_args, mixer_specs, x, mod, g_ffn, w_in, conv_w, conv_b, w_out,
               g_final, final_norm):
    b, t, d = x.shape
    tm = min(ROW_TILE, t)
    fc = D_FF // FF_SPLIT
    assert fc % LANES == 0
    kernel = functools.partial(_ffn_kernel, mixer_out=mixer_out, n_mixer=len(mixer_args),
                               final_norm=final_norm)
    return pl.pallas_call(
        kernel,
        grid=(b, t // tm),
        in_specs=list(mixer_specs(tm)) + [
            pl.BlockSpec((1, tm, d), lambda i, j: (i, j, 0)),
            pl.BlockSpec((1, 6, d), lambda i, j: (i, 0, 0)),
            _resident(1, d),
            _resident(d, 2 * D_FF),
            _resident(3, D_FF),
            _resident(1, D_FF),
            _resident(D_FF, d),
            _resident(1, d),
        ],
        out_specs=pl.BlockSpec((1, tm, d), lambda i, j: (i, j, 0)),
        out_shape=jax.ShapeDtypeStruct((b, t, d), F32),
        scratch_shapes=[
            pltpu.VMEM((FF_SPLIT, tm + CONV_HALO, fc), F32),
            pltpu.VMEM((FF_SPLIT, CONV_HALO, fc), F32),
        ],
        compiler_params=_cparams(("arbitrary", "arbitrary")),
        name="gla_out_ffn_final" if final_norm else "dsa_out_ffn",
    )(*mixer_args, x, mod, g_ffn.reshape(1, d), w_in.astype(BF16), conv_w, conv_b.reshape(1, D_FF),
      w_out.astype(BF16), g_final.reshape(1, d))


def _dsa_specs(tm):
    return (pl.BlockSpec((1, A_HEADS, KV_RANK, tm), lambda i, j: (i, 0, 0, j)),
            _resident(A_HEADS, A_V_DIM, KV_RANK),
            _resident(A_HEADS * A_V_DIM, D_MODEL))


def _gla_specs(tm):
    row = pl.BlockSpec((1, tm, G_VD), lambda i, j: (i, j, 0))
    return (row, row, _resident(1, G_VD), _resident(G_VD, D_MODEL))


def _gla_pre_kernel(x_ref, mod_ref, g_ref, wqkr_ref, wv_ref, wg_ref, wg2_ref, bg_ref,
                    q_ref, k_ref, r_ref, v_ref, la_ref):
    mod = mod_ref[0]
    h = _norm_mod(x_ref[0], g_ref[...], mod[1:2], mod[0:1])
    hb = h.astype(BF16)
    qkr = _dot(hb, wqkr_ref[...])
    q_ref[0] = (qkr[:, :G_KD] * (G_DK ** -0.5)).astype(BF16)
    k_ref[0] = qkr[:, G_KD:2 * G_KD].astype(BF16)
    r_ref[0] = qkr[:, 2 * G_KD:].astype(BF16)
    v_ref[0] = _dot(hb, wv_ref[...]).astype(BF16)
    g_lr = _dot_hp(h, wg_ref[...])
    gate = _dot_hp(g_lr, wg2_ref[...]) + bg_ref[...]
    log_sig = jnp.minimum(gate, 0.0) - jnp.log1p(jnp.exp(-jnp.abs(gate)))
    la_ref[0] = log_sig / GATE_TAU


def _gla_pre(x, mod, g_mix, w_in, w_g2, b_g):
    b, t, d = x.shape
    tm = min(ROW_TILE, t)
    w_q, w_k, w_v, w_r, w_g = jnp.split(w_in, [G_KD, 2 * G_KD, 2 * G_KD + G_VD, 2 * G_KD + 2 * G_VD], axis=1)
    w_qkr = jnp.concatenate([w_q, w_k, w_r], axis=1).astype(BF16)
    w_g_pad = jnp.concatenate([w_g, jnp.zeros((d, LANES - G_RANK), F32)], axis=1)
    w_g2_pad = jnp.concatenate([w_g2, jnp.zeros((LANES - G_RANK, G_KD), F32)], axis=0)
    const = lambda *shape: pl.BlockSpec(shape, lambda i, j: (0,) * len(shape))
    row = lambda width: pl.BlockSpec((1, tm, width), lambda i, j: (i, j, 0))
    return pl.pallas_call(
        _gla_pre_kernel,
        grid=(b, t // tm),
        in_specs=[
            row(d),
            pl.BlockSpec((1, 6, d), lambda i, j: (i, 0, 0)),
            const(1, d),
            const(d, 2 * G_KD + G_VD),
            const(d, G_VD),
            const(d, LANES),
            const(LANES, G_KD),
            const(1, G_KD),
        ],
        out_specs=[row(G_KD), row(G_KD), row(G_VD), row(G_VD), row(G_KD)],
        out_shape=[
            jax.ShapeDtypeStruct((b, t, G_KD), BF16),
            jax.ShapeDtypeStruct((b, t, G_KD), BF16),
            jax.ShapeDtypeStruct((b, t, G_VD), BF16),
            jax.ShapeDtypeStruct((b, t, G_VD), BF16),
            jax.ShapeDtypeStruct((b, t, G_KD), F32),
        ],
        compiler_params=_cparams(("arbitrary", "arbitrary")),
        name="gla_pre",
    )(x, mod, g_mix.reshape(1, d), w_qkr, w_v.astype(BF16), w_g_pad, w_g2_pad, b_g.reshape(1, G_KD))


GLA_LEVELS = tuple(1 << p for p in range(GLA_CHUNK.bit_length() - 1))


def _gla_kernel(q_ref, k_ref, g_ref, v_ref, o_ref, st_scr, sel_scr):
    n_ch = q_ref.shape[1] // GLA_CHUNK
    cl = GLA_CHUNK

    @pl.when(pl.program_id(2) == 0)
    def _():
        st_scr[...] = jnp.zeros(st_scr.shape, F32)

    def lanes(ref):
        x = ref[0]
        return jnp.concatenate([x[c * cl:(c + 1) * cl] for c in range(n_ch)], axis=1)

    q = lanes(q_ref).astype(F32)
    k = lanes(k_ref).astype(F32)
    g_hi, g_lo = _split2(lanes(g_ref) * LOG2E)
    g_pieces = jnp.concatenate([g_hi, g_lo], axis=0)

    first_step = (pl.program_id(0) == 0) & (pl.program_id(1) == 0) & (pl.program_id(2) == 0)

    @pl.when(first_step)
    def _():
        r = lax.broadcasted_iota(I32, ((len(GLA_LEVELS) + 1) * cl, 2 * cl), 0)
        cidx = lax.broadcasted_iota(I32, ((len(GLA_LEVELS) + 1) * cl, 2 * cl), 1) & (cl - 1)
        i_loc = r & (cl - 1)
        lvl = lax.shift_right_logical(r, I32(cl.bit_length() - 1))
        half = lax.shift_left(I32(1), lvl)
        ref_row = jnp.where(lvl < len(GLA_LEVELS), (i_loc & ~(2 * half - 1)) | (half - 1), -1)
        sel = jnp.where(cidx <= i_loc, 1.0, 0.0) - jnp.where(cidx <= ref_row, 1.0, 0.0)
        low_half = (lvl < len(GLA_LEVELS)) & ((i_loc & half) == 0)
        sel_scr[...] = jnp.where(low_half, -sel, sel).astype(BF16)

    d_all = _dot(sel_scr[...], g_pieces)

    row = lax.broadcasted_iota(I32, (cl, 1), 0)
    xor = lax.broadcasted_iota(I32, (cl, cl), 0) ^ lax.broadcasted_iota(I32, (cl, cl), 1)

    a = [jnp.where(xor == 0, _dot_nt(q[:, c * G_DK:(c + 1) * G_DK].astype(BF16),
                                     k[:, c * G_DK:(c + 1) * G_DK].astype(BF16)), 0.0)
         for c in range(n_ch)]
    for li, s in enumerate(GLA_LEVELS):
        dl = d_all[li * cl:(li + 1) * cl]
        hi = (row & s) != 0
        e = jnp.exp2(dl)
        qt = jnp.where(hi, q * e, 0.0).astype(BF16)
        kt = jnp.where(hi, 0.0, k * e).astype(BF16)
        for c in range(n_ch):
            p = _dot_nt(qt[:, c * G_DK:(c + 1) * G_DK], kt[:, c * G_DK:(c + 1) * G_DK])
            a[c] = a[c] + jnp.where(xor < 2 * s, p, 0.0)

    b = d_all[len(GLA_LEVELS) * cl:]
    b_last = b[cl - 1:cl]
    q_hat = (q * jnp.exp2(b)).astype(BF16)
    k_hat = (k * jnp.exp2(b_last - b)).astype(BF16)
    decay = jnp.exp2(b_last)

    st = st_scr[...]
    v_all = v_ref[0]
    outs = []
    for c in range(n_ch):
        sl = slice(c * G_DK, (c + 1) * G_DK)
        v_c = v_all[c * cl:(c + 1) * cl]
        o_inter = _dot_nt(q_hat[:, sl], st.astype(BF16))
        o_intra = _dot(a[c].astype(BF16), v_c)
        outs.append(o_inter + o_intra)
        st = st * decay[:, sl] + lax.dot_general(v_c, k_hat[:, sl], (((0,), (0,)), ((), ())),
                                                 preferred_element_type=F32)
    st_scr[...] = st
    o_ref[0] = jnp.concatenate(outs, axis=0).astype(BF16)


def _gla_core(q, k, la, v):
    b, t, _ = q.shape
    tm = min(GLA_TILE, t)
    assert G_DK == LANES and tm % GLA_CHUNK == 0
    return pl.pallas_call(
        _gla_kernel,
        grid=(b, G_HEADS, t // tm),
        in_specs=[
            pl.BlockSpec((1, tm, G_DK), lambda i, h, j: (i, j, h)),
            pl.BlockSpec((1, tm, G_DK), lambda i, h, j: (i, j, h)),
            pl.BlockSpec((1, tm, G_DK), lambda i, h, j: (i, j, h)),
            pl.BlockSpec((1, tm, G_DV), lambda i, h, j: (i, j, h)),
        ],
        out_specs=pl.BlockSpec((1, tm, G_DV), lambda i, h, j: (i, j, h)),
        out_shape=jax.ShapeDtypeStruct((b, t, G_VD), BF16),
        scratch_shapes=[pltpu.VMEM((G_DV, G_DK), F32),
                        pltpu.VMEM(((len(GLA_LEVELS) + 1) * GLA_CHUNK, 2 * GLA_CHUNK), BF16)],
        compiler_params=_cparams(("arbitrary", "arbitrary", "arbitrary")),
        name="gla_core",
    )(q, k, la, v)


def kernel(x, c, rel_bias, a_w_in, a_g_cq, a_g_ckv, a_w_uq, a_w_uk, a_w_uv, a_w_qi, a_w_o, b_w_in, b_w_g2, b_b_g, b_g_norm, b_w_o, ada_w, ada_b, g_mix, g_ffn, f_w_in, f_conv_w, f_conv_b, f_w_out, g_final):
    b, t, d = x.shape
    depth = ada_w.shape[0]
    assert d == D_MODEL and depth == 2 and t % ROW_TILE == 0
    mod = _modulation(c, ada_w, ada_b).reshape(depth, b, 6, d)

    qlatt, qi3t, ki3, widxt, ckv, ckvt = _dsa_pre(x, mod[0], g_mix[0], a_w_in[0], a_g_cq[0], a_g_ckv[0],
                                                  a_w_uq[0], a_w_uk[0], a_w_qi[0])
    olatt = _dsa_attn(qi3t, ki3, widxt, qlatt, ckv, ckvt, _bias_tiles(rel_bias))
    x = _mixer_ffn(_dsa_out, (olatt, jnp.swapaxes(a_w_uv[0], 1, 2).astype(BF16), a_w_o[0].astype(BF16)),
                   _dsa_specs, x, mod[0], g_ffn[0], f_w_in[0], f_conv_w[0], f_conv_b[0], f_w_out[0],
                   g_final, False)

    q, k, r, v, la = _gla_pre(x, mod[1], g_mix[1], b_w_in[0], b_w_g2[0], b_b_g[0])
    o = _gla_core(q, k, la, v)
    return _mixer_ffn(_gla_out, (o, r, b_g_norm[0].reshape(1, G_VD), b_w_o[0].astype(BF16)),
                      _gla_specs, x, mod[1], g_ffn[1], f_w_in[1], f_conv_w[1], f_conv_b[1], f_w_out[1],
                      g_final, True)
```

```python
import functools
import math

import numpy as np
import jax
import jax.numpy as jnp
from jax import lax
from jax.experimental import pallas as pl
from jax.experimental.pallas import tpu as pltpu

F32 = jnp.float32
BF16 = jnp.bfloat16
I32 = jnp.int32
I16 = jnp.int16

D_MODEL = 1024
A_HEADS = 8
A_HEAD_DIM = 128
A_V_DIM = 128
Q_RANK = 256
KV_RANK = 256
IDX_HEADS = 8
IDX_DIM = 64
TOPK_MAX = 256
NUM_BUCKETS = 32
MAX_DISTANCE = 128
G_HEADS = 4
G_KD = 512
G_VD = 1024
G_DK = G_KD // G_HEADS
G_DV = G_VD // G_HEADS
G_RANK = 16
GATE_TAU = 16.0
D_FF = 2816
EPS = 1e-6

LANES = 128
VMEM_LIMIT_BYTES = 56 * 1024 * 1024

ROW_TILE = 512
ATT_TILE = 256
GLA_TILE = 512
GLA_CHUNK = 128
FF_SPLIT = 2
NEG_INF = float("-inf")
HALF_BIAS = 1 << 15
LOG2E = math.log2(math.e)


def _cparams(sem):
    return pltpu.CompilerParams(dimension_semantics=sem, vmem_limit_bytes=VMEM_LIMIT_BYTES)


def _dot(a, b):
    return lax.dot_general(a, b, (((1,), (0,)), ((), ())), preferred_element_type=F32)


def _dot_nt(a, b):
    return lax.dot_general(a, b, (((1,), (1,)), ((), ())), preferred_element_type=F32)


def _split2(a):
    hi = a.astype(BF16)
    lo = (a - hi.astype(F32)).astype(BF16)
    return hi, lo


def _dot_hp(a, b):
    ah, al = _split2(a)
    bh, bl = _split2(b)
    return _dot(ah, bh) + (_dot(ah, bl) + _dot(al, bh))


def _dot_hp_nt(a, b):
    ah, al = _split2(a)
    bh, bl = _split2(b)
    return _dot_nt(ah, bh) + (_dot_nt(ah, bl) + _dot_nt(al, bh))


def _rms(x):
    return x * lax.rsqrt(jnp.mean(x * x, axis=-1, keepdims=True) + EPS)


def _norm_mod(x, g, sc, sh):
    return (_rms(x) * g) * (1.0 + sc) + sh


def _mod_kernel(c_ref, w_ref, b_ref, o_ref):
    c = c_ref[...]
    cond = c * jax.nn.sigmoid(c)
    o_ref[0] = _dot_hp(cond, w_ref[0]) + b_ref[0]


def _modulation(c, ada_w, ada_b):
    depth, d, six_d = ada_w.shape
    b = c.shape[0]
    n_col = six_d // d
    return pl.pallas_call(
        _mod_kernel,
        grid=(depth, n_col),
        in_specs=[
            pl.BlockSpec((b, d), lambda l, j: (0, 0)),
            pl.BlockSpec((1, d, d), lambda l, j: (l, 0, j)),
            pl.BlockSpec((1, 1, d), lambda l, j: (l, 0, j)),
        ],
        out_specs=pl.BlockSpec((1, b, d), lambda l, j: (l, 0, j)),
        out_shape=jax.ShapeDtypeStruct((depth, b, six_d), F32),
        compiler_params=_cparams(("arbitrary", "arbitrary")),
        name="adaln_mod",
    )(c, ada_w, ada_b.reshape(depth, 1, six_d))


def _rms_cols(xt):
    return xt * lax.rsqrt(jnp.mean(xt * xt, axis=0, keepdims=True) + EPS)


def _dsa_pre_kernel(x_ref, mod_ref, g_ref, wkv_ref, wkvt_ref, wqt_ref, gcq_ref, gckv_ref,
                    gckvc_ref, wuqt_ref, wukt_ref, wqit_ref,
                    qlatt_ref, qi3t_ref, ki3_ref, widxt_ref, ckv_ref, ckvt_ref):
    mod = mod_ref[0]
    h = _norm_mod(x_ref[0], g_ref[...], mod[1:2], mod[0:1])
    hb = h.astype(BF16)

    ckv_ref[0] = (_rms(_dot(hb, wkv_ref[...])) * gckv_ref[...]).astype(BF16)
    c_kvt = (_rms_cols(_dot_nt(wkvt_ref[...], hb)) * gckvc_ref[...]).astype(BF16)
    for j in range(ckvt_ref.shape[1]):
        ckvt_ref[0, j] = c_kvt[:, j * ATT_TILE:(j + 1) * ATT_TILE]

    hpt = _dot_hp_nt(wqt_ref[...], h)
    widxt_ref[0] = hpt[Q_RANK:Q_RANK + IDX_HEADS] * (IDX_HEADS ** -0.5 * IDX_DIM ** -0.5)
    k_idxt = hpt[Q_RANK + IDX_HEADS:]
    k_hi = k_idxt.astype(BF16).astype(F32)
    ki3_ref[0] = jnp.concatenate([k_hi, k_idxt - k_hi, k_hi], axis=0).T.astype(BF16)
    c_qt = _rms_cols(hpt[:Q_RANK]) * gcq_ref[...]
    qt = _dot(wuqt_ref[...], c_qt.astype(BF16))
    for hh in range(A_HEADS):
        qh = qt[hh * A_HEAD_DIM:(hh + 1) * A_HEAD_DIM].astype(BF16)
        qlatt_ref[0, hh] = (_dot(wukt_ref[hh], qh) * (A_HEAD_DIM ** -0.5 * LOG2E)).astype(BF16)
    q_idxt = _dot_hp(wqit_ref[...], c_qt)
    for hh in range(IDX_HEADS):
        piece = q_idxt[hh * IDX_DIM:(hh + 1) * IDX_DIM]
        hi = piece.astype(BF16).astype(F32)
        qi3t_ref[0, hh] = jnp.concatenate([hi, hi, piece - hi], axis=0).astype(BF16)


def _dsa_pre(x, mod, g_mix, w_in, g_cq, g_ckv, w_uq, w_uk, w_qi):
    b, t, d = x.shape
    tm = min(ROW_TILE, t)
    w_cq, w_ckv, w_ki, w_wi = jnp.split(w_in, [Q_RANK, Q_RANK + KV_RANK, Q_RANK + KV_RANK + IDX_DIM], axis=1)
    w_qt = jnp.concatenate([w_cq, w_wi, w_ki], axis=1).T
    const = lambda *shape: pl.BlockSpec(shape, lambda i, j: (0,) * len(shape))
    return pl.pallas_call(
        _dsa_pre_kernel,
        grid=(b, t // tm),
        in_specs=[
            pl.BlockSpec((1, tm, d), lambda i, j: (i, j, 0)),
            pl.BlockSpec((1, 6, d), lambda i, j: (i, 0, 0)),
            const(1, d),
            const(d, KV_RANK),
            const(KV_RANK, d),
            const(Q_RANK + IDX_HEADS + IDX_DIM, d),
            const(Q_RANK, 1),
            const(1, KV_RANK),
            const(KV_RANK, 1),
            const(A_HEADS * A_HEAD_DIM, Q_RANK),
            const(A_HEADS, KV_RANK, A_HEAD_DIM),
            const(IDX_HEADS * IDX_DIM, Q_RANK),
        ],
        out_specs=[
            pl.BlockSpec((1, A_HEADS, KV_RANK, tm), lambda i, j: (i, 0, 0, j)),
            pl.BlockSpec((1, IDX_HEADS, 3 * IDX_DIM, tm), lambda i, j: (i, 0, 0, j)),
            pl.BlockSpec((1, tm, 3 * IDX_DIM), lambda i, j: (i, j, 0)),
            pl.BlockSpec((1, IDX_HEADS, tm), lambda i, j: (i, 0, j)),
            pl.BlockSpec((1, tm, KV_RANK), lambda i, j: (i, j, 0)),
            pl.BlockSpec((1, tm // ATT_TILE, KV_RANK, ATT_TILE), lambda i, j: (i, j, 0, 0)),
        ],
        out_shape=[
            jax.ShapeDtypeStruct((b, A_HEADS, KV_RANK, t), BF16),
            jax.ShapeDtypeStruct((b, IDX_HEADS, 3 * IDX_DIM, t), BF16),
            jax.ShapeDtypeStruct((b, t, 3 * IDX_DIM), BF16),
            jax.ShapeDtypeStruct((b, IDX_HEADS, t), F32),
            jax.ShapeDtypeStruct((b, t, KV_RANK), BF16),
            jax.ShapeDtypeStruct((b, t // ATT_TILE, KV_RANK, ATT_TILE), BF16),
        ],
        compiler_params=_cparams(("arbitrary", "arbitrary")),
        name="dsa_pre",
    )(x, mod, g_mix.reshape(1, d), w_ckv.astype(BF16), w_ckv.T.astype(BF16), w_qt,
      g_cq.reshape(Q_RANK, 1), g_ckv.reshape(1, KV_RANK), g_ckv.reshape(KV_RANK, 1),
      w_uq.T.astype(BF16), jnp.swapaxes(w_uk, 1, 2).astype(BF16), w_qi.T)


def _bucket_starts():
    exact = NUM_BUCKETS // 2
    starts = list(range(1, exact + 1))
    d = exact
    for m in range(exact + 1, NUM_BUCKETS):
        while exact + int(math.log(d / exact) / math.log(MAX_DISTANCE / exact) * (NUM_BUCKETS - exact)) < m:
            d += 1
        starts.append(d)
    return tuple(starts)


BUCKET_STARTS = _bucket_starts()
FAR_DISTANCE = BUCKET_STARTS[-1]


def _bias_kernel(rb_ref, o_ref):
    hh = pl.program_id(0)
    key = lax.broadcasted_iota(I32, (ATT_TILE, ATT_TILE), 0)
    qry = lax.broadcasted_iota(I32, (ATT_TILE, ATT_TILE), 1)
    far = rb_ref[NUM_BUCKETS - 1, hh]
    for kind, offset in ((0, ATT_TILE), (1, 0)):
        dist = jnp.maximum(qry - key + offset, 0)
        acc = jnp.full((ATT_TILE, ATT_TILE), 0.0, F32) + far
        for m in range(NUM_BUCKETS - 2, -1, -1):
            acc = jnp.where(dist < BUCKET_STARTS[m], rb_ref[m, hh], acc)
        o_ref[kind, 0] = (acc - far) * LOG2E


def _bias_tiles(rel_bias):
    assert ATT_TILE >= FAR_DISTANCE
    return pl.pallas_call(
        _bias_kernel,
        grid=(A_HEADS,),
        in_specs=[pl.BlockSpec(memory_space=pltpu.SMEM)],
        out_specs=pl.BlockSpec((2, 1, ATT_TILE, ATT_TILE), lambda h: (0, h, 0, 0)),
        out_shape=jax.ShapeDtypeStruct((2, A_HEADS, ATT_TILE, ATT_TILE), F32),
        compiler_params=_cparams(("arbitrary",)),
        name="t5_bias_tiles",
    )(rel_bias)


def _dsa_attn_kernel(qi3t_ref, ki3_ref, widxt_ref, qlatt_ref, ckv_ref, ckvt_ref, bias_ref, o_ref,
                     s_scr, hi_scr, lo_scr, tie_scr, m_scr, l_scr, acc_scr, *, n_sel, idx_bits):
    qi = pl.program_id(1)
    n_chunks = qi + 1
    tq = ATT_TILE
    q_pos = qi * tq + lax.broadcasted_iota(I32, (1, tq), 1)
    k_loc = lax.broadcasted_iota(I32, (tq, 1), 0)

    w_all = widxt_ref[0]

    n_pairs = lax.shift_right_logical(n_chunks + 1, I32(1))
    k_loc2 = lax.broadcasted_iota(I32, (2 * tq, 1), 0)

    def score_pair(j, carry):
        c = 2 * j
        k3 = ki3_ref[0, pl.ds(pl.multiple_of(c * tq, 2 * tq), 2 * tq), :]
        sc = jnp.zeros((2 * tq, tq), F32)
        for hh in range(IDX_HEADS):
            sc = sc + w_all[hh:hh + 1, :] * jnp.maximum(_dot(k3, qi3t_ref[0, hh]), 0.0)
        sc = jnp.where(c * tq + k_loc2 <= q_pos, sc, NEG_INF)
        s_scr[c] = sc[:tq]
        s_scr[c + 1] = sc[tq:]
        return carry

    lax.fori_loop(0, n_pairs, score_pair, 0)

    def count(pred):
        def body(j, acc):
            c = 2 * j
            ind = (jnp.where(pred(s_scr[c], c * tq + k_loc), 1.0, 0.0).astype(F32)
                   + jnp.where(pred(s_scr[c + 1], (c + 1) * tq + k_loc), 1.0, 0.0).astype(F32))
            return acc + jnp.sum(ind.reshape(tq // 8, 8, tq), axis=0)
        acc = lax.fori_loop(0, n_pairs, body, jnp.zeros((8, tq), F32))
        return jnp.sum(acc, axis=0, keepdims=True)

    def key_to_float(key):
        bits = jnp.where(key < 0, key ^ I32(-2 ** 31), ~key)
        return lax.bitcast_convert_type(bits, F32)

    def split_pair(j, carry):
        for c in (2 * j, 2 * j + 1):
            bits = lax.bitcast_convert_type(s_scr[c], I32)
            bits = jnp.where(bits == I32(-2 ** 31), 0, bits)
            ukey = jnp.where(bits < 0, ~bits, bits ^ I32(-2 ** 31))
            hi_scr[c] = (lax.shift_right_logical(ukey, I32(16)) - HALF_BIAS).astype(I16)
            lo_scr[c] = ((ukey & 0xFFFF) - HALF_BIAS).astype(I16)
        return carry

    lax.fori_loop(0, n_pairs, split_pair, 0)

    def count16(src, pred):
        def body(j, acc):
            c = 2 * j
            ind = (jnp.where(pred(src[c]), I16(1), I16(0)) + jnp.where(pred(src[c + 1]), I16(1), I16(0)))
            rows = ind.reshape(tq // 16, 16, tq)
            part = rows[0]
            for i in range(1, tq // 16):
                part = part + rows[i]
            return acc + part
        acc = lax.fori_loop(0, n_pairs, body, jnp.zeros((16, tq), I16))
        return jnp.sum(acc.astype(I32), axis=0, keepdims=True)

    def half_search(src, target):
        def bit(i, val):
            cand = val | lax.shift_left(I32(1), I32(15) - i)
            cand16 = (cand - HALF_BIAS).astype(I16)
            return jnp.where(count16(src, lambda t: t >= cand16) >= target, cand, val)
        return lax.fori_loop(0, 16, bit, jnp.zeros((1, tq), I32))

    key_hi = half_search(hi_scr, n_sel)
    hi16 = (key_hi - HALF_BIAS).astype(I16)
    n_above = count16(hi_scr, lambda t: t > hi16)

    def keep_matching(j, carry):
        for c in (2 * j, 2 * j + 1):
            lo_scr[c] = jnp.where(hi_scr[c] == hi16, lo_scr[c], I16(-HALF_BIAS))
        return carry

    lax.fori_loop(0, n_pairs, keep_matching, 0)
    key_lo = half_search(lo_scr, n_sel - n_above)
    key = lax.shift_left(key_hi, I32(16)) | key_lo
    fewer = (q_pos + 1) < n_sel
    tau = jnp.where(fewer, NEG_INF, key_to_float(key))

    n_gt = count(lambda s, kp: s > tau)
    need = jnp.where(fewer, 4.0 * ki3_ref.shape[1], n_sel - n_gt)

    def tie_bit(i, x):
        cand = x | lax.shift_left(I32(1), I32(idx_bits - 1) - i)
        cnt = count(lambda s, kp: (s == tau) & (kp < cand))
        return jnp.where(cnt < need, cand, x)

    tie_scr[...] = jnp.full(tie_scr.shape, 2 ** idx_bits - 1, I32)
    n_eq = count(lambda s, kp: s == tau)

    @pl.when(jnp.max(n_eq - need) > 0.0)
    def _():
        tie_scr[...] = lax.fori_loop(0, idx_bits, tie_bit, jnp.zeros((1, tq), I32))

    last_tie = tie_scr[...]

    def mask_chunk(c, carry):
        s = s_scr[c]
        k_pos = c * tq + k_loc
        sel = (k_pos <= q_pos) & ((s > tau) | ((s == tau) & (k_pos <= last_tie)))
        s_scr[c] = jnp.where(sel, 0.0, NEG_INF)
        return carry

    lax.fori_loop(0, n_chunks, mask_chunk, 0)

    m_scr[...] = jnp.full(m_scr.shape, NEG_INF, F32)
    l_scr[...] = jnp.zeros(l_scr.shape, F32)
    acc_scr[...] = jnp.zeros(acc_scr.shape, F32)

    def attend(c0, n_blk, bias_kinds):
        kv = ckv_ref[0, pl.ds(pl.multiple_of(c0 * tq, tq), n_blk * tq), :]
        mask = jnp.concatenate([s_scr[c0 + i] for i in range(n_blk)], axis=0)
        for hh in range(A_HEADS):
            logit = _dot(kv, qlatt_ref[0, hh]) + mask
            if bias_kinds is not None:
                logit = logit + jnp.concatenate([bias_ref[kind, hh] for kind in bias_kinds], axis=0)
            m_old = m_scr[hh:hh + 1]
            m_new = jnp.maximum(m_old, jnp.max(logit, axis=0, keepdims=True))
            m_safe = jnp.where(m_new == NEG_INF, 0.0, m_new)
            alpha = jnp.exp2(m_old - m_safe)
            p = jnp.exp2(logit - m_safe)
            l_scr[hh:hh + 1] = alpha * l_scr[hh:hh + 1] + jnp.sum(p, axis=0, keepdims=True)
            p = p.astype(BF16)
            pv = _dot(ckvt_ref[0, c0], p[:tq])
            for i in range(1, n_blk):
                pv = pv + _dot(ckvt_ref[0, c0 + i], p[i * tq:(i + 1) * tq])
            acc_scr[hh] = alpha * acc_scr[hh] + pv
            m_scr[hh:hh + 1] = m_new

    n_far = jnp.maximum(qi - 1, 0)

    def far_pair(j, carry):
        attend(2 * j, 2, None)
        return carry

    lax.fori_loop(0, lax.shift_right_logical(n_far, I32(1)), far_pair, 0)

    @pl.when((n_far & 1) == 1)
    def _():
        attend(n_far - 1, 1, None)

    @pl.when(qi >= 1)
    def _():
        attend(qi - 1, 2, (0, 1))

    @pl.when(qi == 0)
    def _():
        attend(qi, 1, (1,))

    for hh in range(A_HEADS):
        o_ref[0, hh] = (acc_scr[hh] / l_scr[hh:hh + 1]).astype(BF16)


def _dsa_attn(qi3t, ki3, widxt, qlatt, ckv, ckvt, bias):
    b, t, _ = ckv.shape
    tq = ATT_TILE
    n_sel = min(TOPK_MAX, t // 4)
    assert (t // tq) % 2 == 0
    idx_bits = max(1, (t - 1).bit_length())
    kernel = functools.partial(_dsa_attn_kernel, n_sel=n_sel, idx_bits=idx_bits)
    return pl.pallas_call(
        kernel,
        grid=(b, t // tq),
        in_specs=[
            pl.BlockSpec((1, IDX_HEADS, 3 * IDX_DIM, tq), lambda i, j: (i, 0, 0, j)),
            pl.BlockSpec((1, t, 3 * IDX_DIM), lambda i, j: (i, 0, 0)),
            pl.BlockSpec((1, IDX_HEADS, tq), lambda i, j: (i, 0, j)),
            pl.BlockSpec((1, A_HEADS, KV_RANK, tq), lambda i, j: (i, 0, 0, j)),
            pl.BlockSpec((1, t, KV_RANK), lambda i, j: (i, 0, 0)),
            pl.BlockSpec((1, t // tq, KV_RANK, tq), lambda i, j: (i, 0, 0, 0)),
            pl.BlockSpec((2, A_HEADS, tq, tq), lambda i, j: (0, 0, 0, 0)),
        ],
        out_specs=pl.BlockSpec((1, A_HEADS, KV_RANK, tq), lambda i, j: (i, 0, 0, j)),
        out_shape=jax.ShapeDtypeStruct((b, A_HEADS, KV_RANK, t), BF16),
        scratch_shapes=[
            pltpu.VMEM((t // tq, tq, tq), F32),
            pltpu.VMEM((t // tq, tq, tq), I16),
            pltpu.VMEM((t // tq, tq, tq), I16),
            pltpu.VMEM((1, tq), I32),
            pltpu.VMEM((A_HEADS, tq), F32),
            pltpu.VMEM((A_HEADS, tq), F32),
            pltpu.VMEM((A_HEADS, KV_RANK, tq), F32),
        ],
        compiler_params=_cparams(("arbitrary", "arbitrary")),
        name="dsa_attn",
    )(qi3t, ki3, widxt, qlatt, ckv, ckvt, bias)


CONV_HALO = 8


def _dsa_out(olatt_ref, wuvt_ref, wo_ref):
    heads = [_dot(wuvt_ref[hh], olatt_ref[0, hh]).astype(BF16) for hh in range(A_HEADS)]
    ot = jnp.concatenate(heads, axis=0)
    return lax.dot_general(ot, wo_ref[...], (((0,), (0,)), ((), ())), preferred_element_type=F32)


def _gla_out(o_ref, r_ref, gn_ref, wo_ref):
    o = o_ref[0].astype(F32)
    gn = gn_ref[...]
    heads = [_rms(o[:, hh * G_DV:(hh + 1) * G_DV]) * gn[:, hh * G_DV:(hh + 1) * G_DV]
             for hh in range(G_HEADS)]
    r = r_ref[0].astype(F32)
    z = jnp.concatenate(heads, axis=1) * (r * jax.nn.sigmoid(r))
    return _dot(z.astype(BF16), wo_ref[...])


def _ffn_kernel(*refs, mixer_out, n_mixer, final_norm):
    mixer_refs = refs[:n_mixer]
    (x_ref, mod_ref, g_ref, win_ref, cw_ref, cb_ref, wout_ref, gfin_ref,
     o_ref, u_scr, carry_scr) = refs[n_mixer:]
    tm = x_ref.shape[1]
    fc = u_scr.shape[2]
    mod = mod_ref[0]
    x = x_ref[0] + mod[2:3] * mixer_out(*mixer_refs)
    h = _norm_mod(x, g_ref[...], mod[4:5], mod[3:4]).astype(BF16)

    @pl.when(pl.program_id(1) == 0)
    def _():
        carry_scr[...] = jnp.zeros(carry_scr.shape, F32)

    y = None
    for f in range(FF_SPLIT):
        u = _dot(h, win_ref[:, f * fc:(f + 1) * fc])
        v = _dot(h, win_ref[:, D_FF + f * fc:D_FF + (f + 1) * fc])
        u_scr[f, 0:CONV_HALO] = carry_scr[f]
        u_scr[f, CONV_HALO:CONV_HALO + tm] = u
        carry_scr[f] = u[tm - CONV_HALO:tm]
        cw = cw_ref[:, f * fc:(f + 1) * fc]
        u1 = u_scr[f, CONV_HALO - 1:CONV_HALO - 1 + tm]
        u2 = u_scr[f, CONV_HALO - 2:CONV_HALO - 2 + tm]
        uc = cw[0:1] * u2 + cw[1:2] * u1 + cw[2:3] * u + cb_ref[:, f * fc:(f + 1) * fc]
        act = 0.5 * uc * (1.0 + lax.erf(uc * (2.0 ** -0.5)))
        part = _dot((act * v).astype(BF16), wout_ref[f * fc:(f + 1) * fc, :])
        y = part if y is None else y + part
    out = x + mod[5:6] * y
    if final_norm:
        out = _rms(out) * gfin_ref[...]
    o_ref[0] = out


def _resident(*shape):
    return pl.BlockSpec(shape, lambda i, j: (0,) * len(shape), pipeline_mode=pl.Buffered(1))


def _mixer_ffn(mixer_out, mixer_args, mixer_specs, x, mod, g_ffn, w_in, conv_w, conv_b, w_out,
               g_final, final_norm):
    b, t, d = x.shape
    tm = min(ROW_TILE, t)
    fc = D_FF // FF_SPLIT
    assert fc % LANES == 0
    kernel = functools.partial(_ffn_kernel, mixer_out=mixer_out, n_mixer=len(mixer_args),
                               final_norm=final_norm)
    return pl.pallas_call(
        kernel,
        grid=(b, t // tm),
        in_specs=list(mixer_specs(tm)) + [
            pl.BlockSpec((1, tm, d), lambda i, j: (i, j, 0)),
            pl.BlockSpec((1, 6, d), lambda i, j: (i, 0, 0)),
            _resident(1, d),
            _resident(d, 2 * D_FF),
            _resident(3, D_FF),
            _resident(1, D_FF),
            _resident(D_FF, d),
            _resident(1, d),
        ],
        out_specs=pl.BlockSpec((1, tm, d), lambda i, j: (i, j, 0)),
        out_shape=jax.ShapeDtypeStruct((b, t, d), F32),
        scratch_shapes=[
            pltpu.VMEM((FF_SPLIT, tm + CONV_HALO, fc), F32),
            pltpu.VMEM((FF_SPLIT, CONV_HALO, fc), F32),
        ],
        compiler_params=_cparams(("arbitrary", "arbitrary")),
        name="gla_out_ffn_final" if final_norm else "dsa_out_ffn",
    )(*mixer_args, x, mod, g_ffn.reshape(1, d), w_in.astype(BF16), conv_w, conv_b.reshape(1, D_FF),
      w_out.astype(BF16), g_final.reshape(1, d))


def _dsa_specs(tm):
    return (pl.BlockSpec((1, A_HEADS, KV_RANK, tm), lambda i, j: (i, 0, 0, j)),
            _resident(A_HEADS, A_V_DIM, KV_RANK),
            _resident(A_HEADS * A_V_DIM, D_MODEL))


def _gla_specs(tm):
    row = pl.BlockSpec((1, tm, G_VD), lambda i, j: (i, j, 0))
    return (row, row, _resident(1, G_VD), _resident(G_VD, D_MODEL))


def _gla_pre_kernel(x_ref, mod_ref, g_ref, wqkr_ref, wv_ref, wg_ref, wg2_ref, bg_ref,
                    q_ref, k_ref, r_ref, v_ref, la_ref):
    mod = mod_ref[0]
    h = _norm_mod(x_ref[0], g_ref[...], mod[1:2], mod[0:1])
    hb = h.astype(BF16)
    qkr = _dot(hb, wqkr_ref[...])
    q_ref[0] = (qkr[:, :G_KD] * (G_DK ** -0.5)).astype(BF16)
    k_ref[0] = qkr[:, G_KD:2 * G_KD].astype(BF16)
    r_ref[0] = qkr[:, 2 * G_KD:].astype(BF16)
    v_ref[0] = _dot(hb, wv_ref[...]).astype(BF16)
    g_lr = _dot_hp(h, wg_ref[...])
    gate = _dot_hp(g_lr, wg2_ref[...]) + bg_ref[...]
    log_sig = jnp.minimum(gate, 0.0) - jnp.log1p(jnp.exp(-jnp.abs(gate)))
    la_ref[0] = log_sig / GATE_TAU


def _gla_pre(x, mod, g_mix, w_in, w_g2, b_g):
    b, t, d = x.shape
    tm = min(ROW_TILE, t)
    w_q, w_k, w_v, w_r, w_g = jnp.split(w_in, [G_KD, 2 * G_KD, 2 * G_KD + G_VD, 2 * G_KD + 2 * G_VD], axis=1)
    w_qkr = jnp.concatenate([w_q, w_k, w_r], axis=1).astype(BF16)
    w_g_pad = jnp.concatenate([w_g, jnp.zeros((d, LANES - G_RANK), F32)], axis=1)
    w_g2_pad = jnp.concatenate([w_g2, jnp.zeros((LANES - G_RANK, G_KD), F32)], axis=0)
    const = lambda *shape: pl.BlockSpec(shape, lambda i, j: (0,) * len(shape))
    row = lambda width: pl.BlockSpec((1, tm, width), lambda i, j: (i, j, 0))
    return pl.pallas_call(
        _gla_pre_kernel,
        grid=(b, t // tm),
        in_specs=[
            row(d),
            pl.BlockSpec((1, 6, d), lambda i, j: (i, 0, 0)),
            const(1, d),
            const(d, 2 * G_KD + G_VD),
            const(d, G_VD),
            const(d, LANES),
            const(LANES, G_KD),
            const(1, G_KD),
        ],
        out_specs=[row(G_KD), row(G_KD), row(G_VD), row(G_VD), row(G_KD)],
        out_shape=[
            jax.ShapeDtypeStruct((b, t, G_KD), BF16),
            jax.ShapeDtypeStruct((b, t, G_KD), BF16),
            jax.ShapeDtypeStruct((b, t, G_VD), BF16),
            jax.ShapeDtypeStruct((b, t, G_VD), BF16),
            jax.ShapeDtypeStruct((b, t, G_KD), F32),
        ],
        compiler_params=_cparams(("arbitrary", "arbitrary")),
        name="gla_pre",
    )(x, mod, g_mix.reshape(1, d), w_qkr, w_v.astype(BF16), w_g_pad, w_g2_pad, b_g.reshape(1, G_KD))


GLA_LEVELS = tuple(1 << p for p in range(GLA_CHUNK.bit_length() - 1))


def _gla_kernel(q_ref, k_ref, g_ref, v_ref, o_ref, st_scr, sel_scr):
    n_ch = q_ref.shape[1] // GLA_CHUNK
    cl = GLA_CHUNK

    @pl.when(pl.program_id(2) == 0)
    def _():
        st_scr[...] = jnp.zeros(st_scr.shape, F32)

    def lanes(ref):
        x = ref[0]
        return jnp.concatenate([x[c * cl:(c + 1) * cl] for c in range(n_ch)], axis=1)

    q = lanes(q_ref).astype(F32)
    k = lanes(k_ref).astype(F32)
    g_hi, g_lo = _split2(lanes(g_ref) * LOG2E)
    g_pieces = jnp.concatenate([g_hi, g_lo], axis=0)

    first_step = (pl.program_id(0) == 0) & (pl.program_id(1) == 0) & (pl.program_id(2) == 0)

    @pl.when(first_step)
    def _():
        r = lax.broadcasted_iota(I32, ((len(GLA_LEVELS) + 1) * cl, 2 * cl), 0)
        cidx = lax.broadcasted_iota(I32, ((len(GLA_LEVELS) + 1) * cl, 2 * cl), 1) & (cl - 1)
        i_loc = r & (cl - 1)
        lvl = lax.shift_right_logical(r, I32(cl.bit_length() - 1))
        half = lax.shift_left(I32(1), lvl)
        ref_row = jnp.where(lvl < len(GLA_LEVELS), (i_loc & ~(2 * half - 1)) | (half - 1), -1)
        sel = jnp.where(cidx <= i_loc, 1.0, 0.0) - jnp.where(cidx <= ref_row, 1.0, 0.0)
        low_half = (lvl < len(GLA_LEVELS)) & ((i_loc & half) == 0)
        sel_scr[...] = jnp.where(low_half, -sel, sel).astype(BF16)

    d_all = _dot(sel_scr[...], g_pieces)

    row = lax.broadcasted_iota(I32, (cl, 1), 0)
    xor = lax.broadcasted_iota(I32, (cl, cl), 0) ^ lax.broadcasted_iota(I32, (cl, cl), 1)

    a = [jnp.where(xor == 0, _dot_nt(q[:, c * G_DK:(c + 1) * G_DK].astype(BF16),
                                     k[:, c * G_DK:(c + 1) * G_DK].astype(BF16)), 0.0)
         for c in range(n_ch)]
    for li, s in enumerate(GLA_LEVELS):
        dl = d_all[li * cl:(li + 1) * cl]
        hi = (row & s) != 0
        e = jnp.exp2(dl)
        qt = jnp.where(hi, q * e, 0.0).astype(BF16)
        kt = jnp.where(hi, 0.0, k * e).astype(BF16)
        for c in range(n_ch):
            p = _dot_nt(qt[:, c * G_DK:(c + 1) * G_DK], kt[:, c * G_DK:(c + 1) * G_DK])
            a[c] = a[c] + jnp.where(xor < 2 * s, p, 0.0)

    b = d_all[len(GLA_LEVELS) * cl:]
    b_last = b[cl - 1:cl]
    q_hat = (q * jnp.exp2(b)).astype(BF16)
    k_hat = (k * jnp.exp2(b_last - b)).astype(BF16)
    decay = jnp.exp2(b_last)

    st = st_scr[...]
    v_all = v_ref[0]
    outs = []
    for c in range(n_ch):
        sl = slice(c * G_DK, (c + 1) * G_DK)
        v_c = v_all[c * cl:(c + 1) * cl]
        o_inter = _dot_nt(q_hat[:, sl], st.astype(BF16))
        o_intra = _dot(a[c].astype(BF16), v_c)
        outs.append(o_inter + o_intra)
        st = st * decay[:, sl] + lax.dot_general(v_c, k_hat[:, sl], (((0,), (0,)), ((), ())),
                                                 preferred_element_type=F32)
    st_scr[...] = st
    o_ref[0] = jnp.concatenate(outs, axis=0).astype(BF16)


def _gla_core(q, k, la, v):
    b, t, _ = q.shape
    tm = min(GLA_TILE, t)
    assert G_DK == LANES and tm % GLA_CHUNK == 0
    return pl.pallas_call(
        _gla_kernel,
        grid=(b, G_HEADS, t // tm),
        in_specs=[
            pl.BlockSpec((1, tm, G_DK), lambda i, h, j: (i, j, h)),
            pl.BlockSpec((1, tm, G_DK), lambda i, h, j: (i, j, h)),
            pl.BlockSpec((1, tm, G_DK), lambda i, h, j: (i, j, h)),
            pl.BlockSpec((1, tm, G_DV), lambda i, h, j: (i, j, h)),
        ],
        out_specs=pl.BlockSpec((1, tm, G_DV), lambda i, h, j: (i, j, h)),
        out_shape=jax.ShapeDtypeStruct((b, t, G_VD), BF16),
        scratch_shapes=[pltpu.VMEM((G_DV, G_DK), F32),
                        pltpu.VMEM(((len(GLA_LEVELS) + 1) * GLA_CHUNK, 2 * GLA_CHUNK), BF16)],
        compiler_params=_cparams(("arbitrary", "arbitrary", "arbitrary")),
        name="gla_core",
    )(q, k, la, v)


def kernel(x, c, rel_bias, a_w_in, a_g_cq, a_g_ckv, a_w_uq, a_w_uk, a_w_uv, a_w_qi, a_w_o, b_w_in, b_w_g2, b_b_g, b_g_norm, b_w_o, ada_w, ada_b, g_mix, g_ffn, f_w_in, f_conv_w, f_conv_b, f_w_out, g_final):
    b, t, d = x.shape
    depth = ada_w.shape[0]
    assert d == D_MODEL and depth == 2 and t % ROW_TILE == 0
    mod = _modulation(c, ada_w, ada_b).reshape(depth, b, 6, d)

    qlatt, qi3t, ki3, widxt, ckv, ckvt = _dsa_pre(x, mod[0], g_mix[0], a_w_in[0], a_g_cq[0], a_g_ckv[0],
                                                  a_w_uq[0], a_w_uk[0], a_w_qi[0])
    olatt = _dsa_attn(qi3t, ki3, widxt, qlatt, ckv, ckvt, _bias_tiles(rel_bias))
    x = _mixer_ffn(_dsa_out, (olatt, jnp.swapaxes(a_w_uv[0], 1, 2).astype(BF16), a_w_o[0].astype(BF16)),
                   _dsa_specs, x, mod[0], g_ffn[0], f_w_in[0], f_conv_w[0], f_conv_b[0], f_w_out[0],
                   g_final, False)

    q, k, r, v, la = _gla_pre(x, mod[1], g_mix[1], b_w_in[0], b_w_g2[0], b_b_g[0])
    o = _gla_core(q, k, la, v)
    return _mixer_ffn(_gla_out, (o, r, b_g_norm[0].reshape(1, G_VD), b_w_o[0].astype(BF16)),
                      _gla_specs, x, mod[1], g_ffn[1], f_w_in[1], f_conv_w[1], f_conv_b[1], f_w_out[1],
                      g_final, True)
```
